```python
import math, functools
import jax, jax.numpy as jnp
from jax import lax
import numpy as np

D_MODEL = 1024
BATCH = 1
SEQ = 16384
DEPTH = 1
DEC_BATCH = 128
DEC_SEQ = 1
PAST_LEN = 8192
PAGE_SIZE = 128

GLA_HEADS = 4
GLA_DK = D_MODEL // 2 // GLA_HEADS
GLA_DV = D_MODEL // GLA_HEADS
GLA_LOWRANK = 16
GLA_TAU = 16.0
GLA_CHUNK = 64
MOBA_HEADS = 8
MOBA_DH = D_MODEL // MOBA_HEADS
MOBA_BLOCK = 256
MOBA_TOPK = 3
MOBA_QCHUNK = 32
MOBA_SEQ_GROUP = 16
REL_BUCKETS = 32
REL_MAX_DIST = 128
D_FF = -(-8 * D_MODEL // (3 * 256)) * 256
DN_ALPHA = (2 * DEPTH) ** 0.25
DN_BETA = (8 * DEPTH) ** -0.25
LN_EPS = 1e-5
IN_SPLITS = (GLA_HEADS * GLA_DK, GLA_HEADS * GLA_DK, GLA_HEADS * GLA_DV, GLA_HEADS * GLA_DV, GLA_LOWRANK,
             D_MODEL, D_MODEL, D_MODEL, D_MODEL, D_MODEL)
IN_COLS = sum(IN_SPLITS)

kernel_name = "gla_moba_parallel_deepnorm_step"


def _split_cols(h):
    offs = np.cumsum(IN_SPLITS)[:-1].tolist()
    return jnp.split(h, offs, axis=-1)


def _layer_norm(x, g, b):
    xf = x.astype(jnp.float32)
    mu = jnp.mean(xf, -1, keepdims=True)
    xc = xf - mu
    var = jnp.mean(xc * xc, -1, keepdims=True)
    return (xc * lax.rsqrt(var + LN_EPS) * g.astype(jnp.float32) + b.astype(jnp.float32)).astype(x.dtype)


def _rel_bucket(dist):
    max_exact = REL_BUCKETS // 2
    d = jnp.maximum(dist, 0)
    big = max_exact + (jnp.log(jnp.maximum(d, max_exact).astype(jnp.float32) / max_exact)
                       / math.log(REL_MAX_DIST / max_exact) * (REL_BUCKETS - max_exact)).astype(jnp.int32)
    return jnp.where(d < max_exact, d, jnp.minimum(big, REL_BUCKETS - 1))


def _gla_recurrence(q, k, v, log_a, s0):
    n_b, t, h, _ = q.shape
    dv = v.shape[-1]
    c = math.gcd(t, GLA_CHUNK)
    n = t // c

    def chunks(a):
        return a.astype(jnp.float32).reshape(n_b, n, c, h, a.shape[-1]).swapaxes(0, 1)

    causal = jnp.tril(jnp.ones((c, c), bool))[None, :, :, None, None]

    def step(s, inp):
        qc, kc, vc, gc = inp
        b = jnp.cumsum(gc, axis=1)
        o_inter = jnp.einsum('bthk,bhkv->bthv', qc * jnp.exp(b), s)
        diff = b[:, :, None] - b[:, None, :]
        decay = jnp.exp(jnp.where(causal, diff, -jnp.inf))
        att = jnp.einsum('bthk,bshk,btshk->bhts', qc, kc, decay)
        o_intra = jnp.einsum('bhts,bshv->bthv', att, vc)
        b_last = b[:, -1]
        s_new = s * jnp.exp(b_last)[..., None] + jnp.einsum(
            'bshk,bshv->bhkv', kc * jnp.exp(b_last[:, None] - b), vc)
        return s_new, o_inter + o_intra

    s_fin, o = lax.scan(step, s0.astype(jnp.float32), (chunks(q), chunks(k), chunks(v), chunks(log_a)))
    return o.swapaxes(0, 1).reshape(n_b, t, h, dv), s_fin


def _moba_attend(q, q_pos, means, fetch, rel_bias):
    n_b, nq, h, dh = q.shape
    nb = means.shape[1]
    own = q_pos // MOBA_BLOCK
    scores = jnp.einsum('nqhd,nbhd->nqhb', q.astype(jnp.float32), means)
    past = jnp.arange(nb)[None, :] < own[:, None]
    scores = jnp.where(past[None, :, None, :], scores, -jnp.inf)
    _, sel = lax.top_k(scores, MOBA_TOPK)
    offs = jnp.arange(MOBA_BLOCK)
    sel_pos = (sel[..., None] * MOBA_BLOCK + offs).reshape(n_b, nq, h, MOBA_TOPK * MOBA_BLOCK)
    own_rows = own[:, None] * MOBA_BLOCK + offs
    own_pos = jnp.broadcast_to(own_rows[None, :, None, :], (n_b, nq, h, MOBA_BLOCK))
    k_pos = jnp.concatenate([sel_pos, own_pos], -1)
    sel_ok = jnp.repeat(jnp.arange(MOBA_TOPK)[None, :] < own[:, None], MOBA_BLOCK, axis=1)
    own_ok = own_rows <= q_pos[:, None]
    valid = jnp.concatenate([sel_ok, own_ok], -1)[None, :, None, :]
    kg, vg = fetch(k_pos)
    logits = jnp.einsum('nqhd,nqhsd->nqhs', q, kg, preferred_element_type=jnp.float32) * (dh ** -0.5)
    bias = rel_bias[_rel_bucket(q_pos[None, :, None, None] - k_pos), jnp.arange(h)[None, None, :, None]]
    logits = jnp.where(valid, logits + bias.astype(jnp.float32), -jnp.inf)
    p = jax.nn.softmax(logits, axis=-1)
    out = jnp.einsum('nqhs,nqhsd->nqhd', p.astype(vg.dtype), vg, preferred_element_type=jnp.float32)
    return out.astype(q.dtype)


def _moba_sweep(q, q_pos, means, fetch, rel_bias):
    n_b, t, h, dh = q.shape
    c = math.gcd(t, MOBA_QCHUNK)
    n = t // c
    qs = q.reshape(n_b, n, c, h, dh).swapaxes(0, 1)
    ps = q_pos.reshape(n, c)
    out = lax.map(lambda a: _moba_attend(a[0], a[1], means, fetch, rel_bias), (qs, ps))
    return out.swapaxes(0, 1).reshape(n_b, t, h, dh)


def _moba_prompt(q, k, v, rel_bias):
    n_b, t, h, dh = q.shape
    nb = max(-(-t // MOBA_BLOCK), MOBA_TOPK)
    kp = jnp.pad(k.astype(jnp.float32), ((0, 0), (0, nb * MOBA_BLOCK - t), (0, 0), (0, 0)))
    means = kp.reshape(n_b, nb, MOBA_BLOCK, h, dh).sum(2) / MOBA_BLOCK
    bi = jnp.arange(n_b)[:, None, None, None]
    hi = jnp.arange(h)[None, None, :, None]

    def fetch(pos):
        p = jnp.clip(pos, 0, t - 1)
        return k[bi, p, hi], v[bi, p, hi]

    return _moba_sweep(q, jnp.arange(t), means, fetch, rel_bias)


def _moba_sample(q, k_new, v_new, rel_bias, cache_k, cache_v, layer, page_sums, page_table):
    n_b, t, h, dh = q.shape
    n_pages = page_table.shape[1]
    past = n_pages * PAGE_SIZE
    nb = max(-(-(past + t) // MOBA_BLOCK), MOBA_TOPK)
    ppb = MOBA_BLOCK // PAGE_SIZE
    seq_sums = page_sums[page_table]
    seq_sums = jnp.pad(seq_sums, ((0, 0), (0, nb * ppb - n_pages), (0, 0), (0, 0)))
    blk = seq_sums.reshape(n_b, nb, ppb, h, dh).sum(2)
    q_pos = past + jnp.arange(t)
    new_blk = jax.nn.one_hot(q_pos // MOBA_BLOCK, nb, dtype=jnp.float32)
    means = (blk + jnp.einsum('sn,bshd->bnhd', new_blk, k_new.astype(jnp.float32))) / MOBA_BLOCK
    g = math.gcd(n_b, MOBA_SEQ_GROUP)
    ng = n_b // g
    bi = jnp.arange(g)[:, None, None, None]
    hi = jnp.arange(h)[None, None, :, None]

    def grp(a):
        return a.reshape((ng, g) + a.shape[1:])

    def one_group(args):
        qg, mg, ptg, kng, vng = args

        def fetch(pos):
            in_cache = (pos < past)[..., None]
            pc = jnp.clip(pos, 0, past - 1)
            phys = ptg[bi, pc // PAGE_SIZE]
            off = pc % PAGE_SIZE
            pn = jnp.clip(pos - past, 0, t - 1)
            kk = jnp.where(in_cache, cache_k[layer, phys, off, hi], kng[bi, pn, hi].astype(cache_k.dtype))
            vv = jnp.where(in_cache, cache_v[layer, phys, off, hi], vng[bi, pn, hi].astype(cache_v.dtype))
            return kk, vv

        return _moba_sweep(qg, q_pos, mg, fetch, rel_bias)

    out = lax.map(one_group, (grp(q), grp(means), grp(page_table), grp(k_new), grp(v_new)))
    return out.reshape(n_b, t, h, dh)


def _block(x, layer, s0, moba_fn, w_in, w_alpha2, b_alpha, g_gla_norm, w_out,
           ln1_g, ln1_b, w_ffn_in, w_ffn_out, ln2_g, ln2_b):
    n_b, t, _ = x.shape
    q_a, k_a, v_a, r_a, lr_a, q_b, k_b, v_b, g_a, g_b = _split_cols(x @ w_in[layer])

    def hd(a, nh):
        return a.reshape(n_b, t, nh, -1)

    log_a = jax.nn.log_sigmoid((lr_a @ w_alpha2[layer] + b_alpha[layer]).astype(jnp.float32)) / GLA_TAU
    o_a, s_fin = _gla_recurrence(hd(q_a, GLA_HEADS) * (GLA_DK ** -0.5), hd(k_a, GLA_HEADS),
                                 hd(v_a, GLA_HEADS), hd(log_a, GLA_HEADS), s0)
    o_a = o_a * lax.rsqrt(jnp.mean(o_a * o_a, -1, keepdims=True) + LN_EPS) * g_gla_norm[layer].astype(jnp.float32)
    o_a = o_a.reshape(n_b, t, D_MODEL).astype(x.dtype) * jax.nn.silu(r_a)
    k_b = hd(k_b, MOBA_HEADS)
    v_b = hd(v_b, MOBA_HEADS)
    o_b = moba_fn(hd(q_b, MOBA_HEADS), k_b, v_b).reshape(n_b, t, D_MODEL)
    mix = (jax.nn.sigmoid(g_a) * o_a + jax.nn.sigmoid(g_b) * o_b) @ w_out[layer]
    x = _layer_norm(DN_ALPHA * x + mix, ln1_g[layer], ln1_b[layer])
    gate, up = jnp.split(x @ w_ffn_in[layer], 2, axis=-1)
    x = _layer_norm(DN_ALPHA * x + (jax.nn.silu(gate) * up) @ w_ffn_out[layer], ln2_g[layer], ln2_b[layer])
    return x, k_b, v_b, s_fin


def setup_inputs(seed: int = 0) -> dict:
    key = jax.random.key(seed)
    ks = jax.random.split(key, 20)
    n_pages = PAST_LEN // PAGE_SIZE
    n_phys = (DEC_BATCH * n_pages * 5) // 4
    col_scale = np.concatenate([np.full(n, s, np.float32) for n, s in zip(
        IN_SPLITS, (1.0, 1.0, DN_BETA, 1.0, 1.0, 1.0, 1.0, DN_BETA, 1.0, 1.0))])
    nrm = jax.random.normal
    return {
        'x_prompt': nrm(ks[0], (BATCH, SEQ, D_MODEL), jnp.float32),
        'x_sample': nrm(ks[1], (DEC_BATCH, DEC_SEQ, D_MODEL), jnp.float32),
        'cache_k': nrm(ks[2], (DEPTH, n_phys, PAGE_SIZE, MOBA_HEADS, MOBA_DH), jnp.float32),
        'cache_v': nrm(ks[3], (DEPTH, n_phys, PAGE_SIZE, MOBA_HEADS, MOBA_DH), jnp.float32) * DN_BETA,
        'state_gla': nrm(ks[4], (DEPTH, DEC_BATCH, GLA_HEADS, GLA_DK, GLA_DV), jnp.float32),
        'page_table': jax.random.permutation(ks[5], n_phys)[: DEC_BATCH * n_pages].reshape(DEC_BATCH, n_pages).astype(jnp.int32),
        'w_in': nrm(ks[6], (DEPTH, D_MODEL, IN_COLS), jnp.float32) * (D_MODEL ** -0.5) * jnp.asarray(col_scale),
        'w_alpha2': nrm(ks[7], (DEPTH, GLA_LOWRANK, GLA_HEADS * GLA_DK), jnp.float32) * (GLA_LOWRANK ** -0.5),
        'b_alpha': nrm(ks[8], (DEPTH, GLA_HEADS * GLA_DK), jnp.float32) * 0.1,
        'g_gla_norm': 1.0 + 0.02 * nrm(ks[9], (DEPTH, GLA_DV), jnp.float32),
        'rel_bias': 0.1 * nrm(ks[10], (REL_BUCKETS, MOBA_HEADS), jnp.float32),
        'w_out': nrm(ks[11], (DEPTH, D_MODEL, D_MODEL), jnp.float32) * (D_MODEL ** -0.5) * DN_BETA,
        'ln1_g': 1.0 + 0.02 * nrm(ks[12], (DEPTH, D_MODEL), jnp.float32),
        'ln1_b': 0.02 * nrm(ks[13], (DEPTH, D_MODEL), jnp.float32),
        'w_ffn_in': nrm(ks[14], (DEPTH, D_MODEL, 2 * D_FF), jnp.float32) * (D_MODEL ** -0.5),
        'w_ffn_out': nrm(ks[15], (DEPTH, D_FF, D_MODEL), jnp.float32) * (D_FF ** -0.5) * DN_BETA,
        'ln2_g': 1.0 + 0.02 * nrm(ks[16], (DEPTH, D_MODEL), jnp.float32),
        'ln2_b': 0.02 * nrm(ks[17], (DEPTH, D_MODEL), jnp.float32),
    }


def reference(x_prompt, x_sample, cache_k, cache_v, state_gla, page_table, w_in, w_alpha2, b_alpha,
              g_gla_norm, rel_bias, w_out, ln1_g, ln1_b, w_ffn_in, w_ffn_out, ln2_g, ln2_b):
    weights = (w_in, w_alpha2, b_alpha, g_gla_norm, w_out, ln1_g, ln1_b, w_ffn_in, w_ffn_out, ln2_g, ln2_b)
    page_sums = jnp.sum(cache_k, axis=2, dtype=jnp.float32)
    s0_prompt = jnp.zeros((x_prompt.shape[0], GLA_HEADS, GLA_DK, GLA_DV), jnp.float32)
    yp, ys = x_prompt, x_sample
    kp_l, vp_l, sp_l, ks_l, vs_l, ss_l = [], [], [], [], [], []
    for layer in range(DEPTH):
        prompt_moba = functools.partial(_moba_prompt, rel_bias=rel_bias)
        sample_moba = functools.partial(_moba_sample, rel_bias=rel_bias, cache_k=cache_k, cache_v=cache_v,
                                        layer=layer, page_sums=page_sums[layer], page_table=page_table)
        yp, kp, vp, sp = _block(yp, layer, s0_prompt, prompt_moba, *weights)
        ys, kn, vn, sn = _block(ys, layer, state_gla[layer], sample_moba, *weights)
        kp_l.append(kp); vp_l.append(vp); sp_l.append(sp)
        ks_l.append(kn); vs_l.append(vn); ss_l.append(sn)
    return (yp, ys,
            jnp.stack(kp_l).astype(cache_k.dtype), jnp.stack(vp_l).astype(cache_v.dtype),
            jnp.stack(sp_l).astype(state_gla.dtype),
            jnp.stack(ks_l).astype(cache_k.dtype), jnp.stack(vs_l).astype(cache_v.dtype),
            jnp.stack(ss_l).astype(state_gla.dtype))
```

```python
import functools
import math

import jax
import jax.numpy as jnp
import numpy as np
from jax import lax
from jax.experimental import pallas as pl
from jax.experimental.pallas import tpu as pltpu

D_MODEL = 1024
DEPTH = 1
GLA_HEADS = 4
GLA_DK = 128
GLA_DV = 256
GLA_LOWRANK = 16
GLA_TAU = 16.0
GLA_CHUNK = 64
MOBA_HEADS = 8
MOBA_DH = 128
MOBA_BLOCK = 256
MOBA_TOPK = 3
PAGE_SIZE = 128
REL_BUCKETS = 32
REL_MAX_DIST = 128
DN_ALPHA = (2 * DEPTH) ** 0.25
LN_EPS = 1e-5

LANE = 128
SUBLANE = 8
VMEM_LIMIT_BYTES = 56 * 1024 * 1024

NEG = -1e30
BIG = 1e30

ROW_TILE = 256
GLA_STEP = 256
GLA_DECODE_SEQS = 8
PAGES_PER_STEP = 16

F32 = jnp.float32
BF16 = jnp.bfloat16


def _cparams(n_axes):
    return pltpu.CompilerParams(dimension_semantics=("arbitrary",) * n_axes,
                                vmem_limit_bytes=VMEM_LIMIT_BYTES)


def _resident(shape):
    return pl.BlockSpec(shape, lambda *_: (0,) * len(shape), pipeline_mode=pl.Buffered(1))


def _split3(a):
    hi = a.astype(BF16)
    r = a - hi.astype(F32)
    mid = r.astype(BF16)
    lo = (r - mid.astype(F32)).astype(BF16)
    return hi, mid, lo


def _dot(a, b):
    return jnp.dot(a, b, preferred_element_type=F32)


def _dot_f32ish(a, b):
    ah, al, _ = _split3(a)
    bh, bl, _ = _split3(b)
    return _dot(ah, bh) + (_dot(ah, bl) + _dot(al, bh))


def _log_sigmoid(z):
    return jnp.minimum(z, 0.0) - jnp.log1p(jnp.exp(-jnp.abs(z)))


def _silu(x):
    return x * jax.nn.sigmoid(x)


def _layer_norm(x, g, b):
    mu = jnp.mean(x, -1, keepdims=True)
    xc = x - mu
    var = jnp.mean(xc * xc, -1, keepdims=True)
    return xc * lax.rsqrt(var + LN_EPS) * g + b


def _in_proj_kernel(x_ref, w_ref, wlr_ref, *out_refs):
    xb = x_ref[...].astype(BF16)
    for g, o_ref in enumerate(out_refs[:-1]):
        o_ref[...] = _dot(xb, w_ref[:, g * D_MODEL:(g + 1) * D_MODEL])
    out_refs[-1][...] = _dot(xb, wlr_ref[...])


def _in_proj(x2d, w_main, w_lr):
    rows = x2d.shape[0]
    tm = min(ROW_TILE, rows)
    groups = w_main.shape[1] // D_MODEL
    row_block = lambda width: pl.BlockSpec((tm, width), lambda i: (i, 0))
    return pl.pallas_call(
        _in_proj_kernel,
        grid=(rows // tm,),
        in_specs=[row_block(D_MODEL), _resident(w_main.shape), _resident(w_lr.shape)],
        out_specs=[row_block(D_MODEL)] * groups + [row_block(LANE)],
        out_shape=[jax.ShapeDtypeStruct((rows, D_MODEL), F32)] * groups
        + [jax.ShapeDtypeStruct((rows, LANE), F32)],
        compiler_params=_cparams(1),
        name="in_proj",
    )(x2d, w_main, w_lr)


def _out_ffn_kernel(x_ref, oa_ref, ob_ref, ga_ref, gb_ref, wo_ref, w1_ref, w2_ref,
                    ln1g_ref, ln1b_ref, ln2g_ref, ln2b_ref, y_ref):
    d_ff = w2_ref.shape[0]
    merged = jax.nn.sigmoid(ga_ref[...]) * oa_ref[...] + jax.nn.sigmoid(gb_ref[...]) * ob_ref[...]
    mix = _dot(merged.astype(BF16), wo_ref[...])
    x1 = _layer_norm(DN_ALPHA * x_ref[...] + mix, ln1g_ref[...], ln1b_ref[...])
    hidden = _dot(x1.astype(BF16), w1_ref[...])
    act = _silu(hidden[:, :d_ff]) * hidden[:, d_ff:]
    ffn = _dot(act.astype(BF16), w2_ref[...])
    y_ref[...] = _layer_norm(DN_ALPHA * x1 + ffn, ln2g_ref[...], ln2b_ref[...])


def _out_ffn(x2d, o_a, o_b, g_a, g_b, wo, w1, w2, ln1g, ln1b, ln2g, ln2b):
    rows = x2d.shape[0]
    tm = min(ROW_TILE, rows)
    row_block = pl.BlockSpec((tm, D_MODEL), lambda i: (i, 0))
    vec = _resident((1, D_MODEL))
    return pl.pallas_call(
        _out_ffn_kernel,
        grid=(rows // tm,),
        in_specs=[row_block] * 5 + [_resident(wo.shape), _resident(w1.shape), _resident(w2.shape)]
        + [vec] * 4,
        out_specs=row_block,
        out_shape=jax.ShapeDtypeStruct((rows, D_MODEL), F32),
        compiler_params=_cparams(1),
        name="out_ffn",
    )(x2d, o_a, o_b, g_a, g_b, wo, w1, w2, ln1g, ln1b, ln2g, ln2b)


def _gla_gates(lr, wa2_ref, ba_ref):
    z = _dot(lr.astype(BF16), wa2_ref[...]) + ba_ref[...]
    return _log_sigmoid(z) / GLA_TAU


def _gla_out_norm(o, g, r):
    o = o * lax.rsqrt(jnp.mean(o * o, -1, keepdims=True) + LN_EPS) * g
    return o * _silu(r)


def _gla_prompt_kernel(qk_ref, v_ref, r_ref, lr_ref, wa2_ref, ba_ref, g_ref, o_ref, s_ref):
    c = GLA_CHUNK
    hk = GLA_HEADS * GLA_DK

    @pl.when(pl.program_id(0) == 0)
    def _():
        s_ref[...] = jnp.zeros(s_ref.shape, F32)

    row = lax.broadcasted_iota(jnp.int32, (c, c), 0)
    col = lax.broadcasted_iota(jnp.int32, (c, c), 1)
    causal = row >= col
    tri = jnp.where(causal, 1.0, 0.0).astype(BF16)
    for ci in range(GLA_STEP // c):
        rows = slice(ci * c, (ci + 1) * c)
        la = _gla_gates(lr_ref[rows, :], wa2_ref, ba_ref)
        la_hi, la_mid, la_lo = _split3(la)
        b = _dot(tri, la_hi) + (_dot(tri, la_mid) + _dot(tri, la_lo))
        for h in range(GLA_HEADS):
            kcols = slice(h * GLA_DK, (h + 1) * GLA_DK)
            vcols = slice(h * GLA_DV, (h + 1) * GLA_DV)
            bh = b[:, kcols]
            qh = qk_ref[rows, kcols] * (GLA_DK ** -0.5)
            kh = qk_ref[rows, hk + h * GLA_DK: hk + (h + 1) * GLA_DK]
            vh = v_ref[rows, vcols].astype(BF16)
            s_old = s_ref[h]
            qt = (qh * jnp.exp(bh)).astype(BF16)
            b_t = bh.T
            k_t = kh.T
            kt_t = (k_t * jnp.exp(-b_t)).astype(BF16)
            att = jnp.where(causal, _dot(qt, kt_t), 0.0)
            o = _dot(qt, s_old.astype(BF16)) + _dot(att.astype(BF16), vh)
            b_last = b_t[:, c - 1:c]
            kd_t = (k_t * jnp.exp(b_last - b_t)).astype(BF16)
            s_ref[h] = s_old * jnp.exp(b_last) + _dot(kd_t, vh)
            o_ref[rows, vcols] = _gla_out_norm(o, g_ref[...], r_ref[rows, vcols])


def _gla_prompt(qk, v, r, lr, wa2, ba, g):
    t = qk.shape[0]
    step = lambda width: pl.BlockSpec((GLA_STEP, width), lambda i: (i, 0))
    state_shape = (GLA_HEADS, GLA_DK, GLA_DV)
    return pl.pallas_call(
        _gla_prompt_kernel,
        grid=(t // GLA_STEP,),
        in_specs=[step(D_MODEL), step(D_MODEL), step(D_MODEL), step(LANE),
                  _resident(wa2.shape), _resident(ba.shape), _resident(g.shape)],
        out_specs=[step(D_MODEL), pl.BlockSpec(state_shape, lambda i: (0, 0, 0))],
        out_shape=[jax.ShapeDtypeStruct((t, D_MODEL), F32), jax.ShapeDtypeStruct(state_shape, F32)],
        compiler_params=_cparams(1),
        name="gla_prompt",
    )(qk, v, r, lr, wa2, ba, g)


def _gla_decode_kernel(qk_ref, v_ref, r_ref, lr_ref, wa2_ref, ba_ref, g_ref, s0_ref,
                       o_ref, s_ref, o_scr):
    hk = GLA_HEADS * GLA_DK
    a = jnp.exp(_gla_gates(lr_ref[...], wa2_ref, ba_ref))
    for h in range(GLA_HEADS):
        kcols = slice(h * GLA_DK, (h + 1) * GLA_DK)
        vcols = slice(h * GLA_DV, (h + 1) * GLA_DV)
        a_t = a[:, kcols].T
        q_t = (qk_ref[:, kcols] * (GLA_DK ** -0.5)).T
        k_t = qk_ref[:, hk + h * GLA_DK: hk + (h + 1) * GLA_DK].T
        for n in range(GLA_DECODE_SEQS):
            s_new = s0_ref[n, h] * a_t[:, n:n + 1] + k_t[:, n:n + 1] * v_ref[n:n + 1, vcols]
            s_ref[n, h] = s_new
            o_scr[n:n + 1, vcols] = jnp.sum(q_t[:, n:n + 1] * s_new, axis=0, keepdims=True)
    for h in range(GLA_HEADS):
        vcols = slice(h * GLA_DV, (h + 1) * GLA_DV)
        o_ref[:, vcols] = _gla_out_norm(o_scr[:, vcols], g_ref[...], r_ref[:, vcols])


def _gla_decode(qk, v, r, lr, wa2, ba, g, s0):
    n = qk.shape[0]
    nb = GLA_DECODE_SEQS
    rows = lambda width: pl.BlockSpec((nb, width), lambda i: (i, 0))
    state = pl.BlockSpec((nb, GLA_HEADS, GLA_DK, GLA_DV), lambda i: (i, 0, 0, 0))
    return pl.pallas_call(
        _gla_decode_kernel,
        grid=(n // nb,),
        in_specs=[rows(D_MODEL), rows(D_MODEL), rows(D_MODEL), rows(LANE),
                  _resident(wa2.shape), _resident(ba.shape), _resident(g.shape), state],
        out_specs=[rows(D_MODEL), state],
        out_shape=[jax.ShapeDtypeStruct((n, D_MODEL), F32), jax.ShapeDtypeStruct(s0.shape, F32)],
        scratch_shapes=[pltpu.VMEM((nb, D_MODEL), F32)],
        compiler_params=_cparams(1),
        name="gla_decode",
    )(qk, v, r, lr, wa2, ba, g, s0)


def _moba_prep_kernel(k_ref, v_ref, k16_ref, vt16_ref, mean_ref):
    k = k_ref[...]
    k16_ref[...] = k.astype(BF16)
    mean_ref[...] = jnp.sum(k, axis=0, keepdims=True) / MOBA_BLOCK
    vt16_ref[...] = v_ref[...].T.astype(BF16)


def _moba_prep(k, v):
    t = k.shape[0]
    nb = t // MOBA_BLOCK
    blk = pl.BlockSpec((MOBA_BLOCK, D_MODEL), lambda i: (i, 0))
    k16, vt16, means = pl.pallas_call(
        _moba_prep_kernel,
        grid=(nb,),
        in_specs=[blk, blk],
        out_specs=[blk, pl.BlockSpec((D_MODEL, MOBA_BLOCK), lambda i: (0, i)),
                   pl.BlockSpec((None, 1, D_MODEL), lambda i: (i, 0, 0))],
        out_shape=[jax.ShapeDtypeStruct((t, D_MODEL), BF16), jax.ShapeDtypeStruct((D_MODEL, t), BF16),
                   jax.ShapeDtypeStruct((nb, 1, D_MODEL), F32)],
        compiler_params=_cparams(1),
        name="moba_prep",
    )(k, v)
    return k16, vt16, means.reshape(nb, D_MODEL)


def _select_topk_rows(scores, valid):
    n = scores.shape[0]
    blk = lax.broadcasted_iota(jnp.int32, scores.shape, 0)
    sc = jnp.where(valid, scores, -jnp.inf)
    sel = jnp.zeros(scores.shape, F32)
    picks = []
    for _ in range(MOBA_TOPK):
        mx = jnp.max(sc, axis=0, keepdims=True)
        idx = jnp.min(jnp.where(sc == mx, blk, n), axis=0, keepdims=True)
        hit = blk == idx
        sel = jnp.where(jnp.logical_and(hit, valid), 1.0, sel)
        sc = jnp.where(hit, -jnp.inf, sc)
        picks.append(idx)
    return sel, picks


def _moba_prompt_kernel(cfar_ref, q_ref, k_ref, vt_ref, mean_ref, bnear_ref, bdiag_ref,
                        o_ref, sel_ref, acc_ref):
    h = pl.program_id(0)
    i = pl.program_id(1)
    blk_q = MOBA_BLOCK
    q_t = q_ref[...].T
    scores = _dot_f32ish(mean_ref[...], q_t)
    blk = lax.broadcasted_iota(jnp.int32, scores.shape, 0)
    sel, _ = _select_topk_rows(scores, blk < i)
    sel_ref[...] = sel
    qs = (q_t * (MOBA_DH ** -0.5)).astype(BF16)
    acc_ref[...] = jnp.zeros(acc_ref.shape, F32)

    def update(carry, j, selected, bias, shift_const):
        m_old, l_old = carry
        off = pl.multiple_of(j * MOBA_BLOCK, MOBA_BLOCK)
        s = _dot(k_ref[pl.ds(off, MOBA_BLOCK), :], qs)
        if bias is not None:
            s = s + bias
        cm = jnp.max(s, axis=0, keepdims=True) + shift_const
        m_new = jnp.where(selected, jnp.maximum(m_old, cm), m_old)
        alpha = jnp.exp(m_old - m_new)
        p = jnp.exp(s - jnp.where(selected, m_new - shift_const, BIG))
        l_new = alpha * l_old + jnp.sum(p, axis=0, keepdims=True)
        pv = _dot(vt_ref[:, pl.ds(off, MOBA_BLOCK)], p.astype(BF16))
        acc_ref[...] = alpha * acc_ref[...] + pv
        return m_new, l_new

    def sel_row(j):
        return sel_ref[pl.ds(j, 1), :] > 0.5

    carry = (jnp.full((1, blk_q), NEG, F32), jnp.zeros((1, blk_q), F32))
    cfar = cfar_ref[h]
    carry = lax.fori_loop(0, jnp.maximum(i - 1, 0),
                          lambda j, c: update(c, j, sel_row(j), None, cfar), carry)
    j_near = jnp.maximum(i - 1, 0)
    carry = update(carry, j_near, sel_row(j_near), bnear_ref[...], 0.0)
    _, l_fin = update(carry, i, True, bdiag_ref[...], 0.0)
    o_ref[...] = (acc_ref[...] / l_fin).T


def _moba_prompt(q, k16, vt16, means, bnear_t, bdiag_t, cfar):
    t = q.shape[0]
    nb = t // MOBA_BLOCK
    bias_spec = pl.BlockSpec((None, MOBA_BLOCK, MOBA_BLOCK), lambda h, i: (h, 0, 0))
    return pl.pallas_call(
        _moba_prompt_kernel,
        grid=(MOBA_HEADS, nb),
        in_specs=[pl.BlockSpec(memory_space=pltpu.SMEM),
                  pl.BlockSpec((MOBA_BLOCK, MOBA_DH), lambda h, i: (i, h)),
                  pl.BlockSpec((t, MOBA_DH), lambda h, i: (0, h)),
                  pl.BlockSpec((MOBA_DH, t), lambda h, i: (h, 0)),
                  pl.BlockSpec((nb, MOBA_DH), lambda h, i: (0, h)),
                  bias_spec, bias_spec],
        out_specs=pl.BlockSpec((MOBA_BLOCK, MOBA_DH), lambda h, i: (i, h)),
        out_shape=jax.ShapeDtypeStruct((t, D_MODEL), F32),
        scratch_shapes=[pltpu.VMEM((nb, MOBA_BLOCK), F32), pltpu.VMEM((MOBA_DH, MOBA_BLOCK), F32)],
        compiler_params=_cparams(2),
        name="moba_prompt",
    )(cfar, q, k16, vt16, means, bnear_t, bdiag_t)


def _block_sums_kernel(pt_ref, cache_ref, out_ref, buf, sems, *, layer):
    step = pl.program_id(0)
    slot = step % 2
    pages_per_block = MOBA_BLOCK // PAGE_SIZE

    def page_copies(which_step, which_slot):
        for u in range(PAGES_PER_STEP):
            page = pt_ref[which_step * PAGES_PER_STEP + u]
            yield pltpu.make_async_copy(cache_ref.at[layer, page], buf.at[which_slot, u], sems.at[which_slot])

    @pl.when(step == 0)
    def _():
        for cp in page_copies(step, slot):
            cp.start()

    @pl.when(step + 1 < pl.num_programs(0))
    def _():
        for cp in page_copies(step + 1, 1 - slot):
            cp.start()

    for cp in page_copies(step, slot):
        cp.wait()

    for b in range(PAGES_PER_STEP // pages_per_block):
        total = jnp.sum(buf[slot, b * pages_per_block], axis=0)
        for u in range(1, pages_per_block):
            total = total + jnp.sum(buf[slot, b * pages_per_block + u], axis=0)
        out_ref[b] = total


def _block_sums(cache, layer, page_table):
    n, n_pages = page_table.shape
    blocks_per_step = PAGES_PER_STEP // (MOBA_BLOCK // PAGE_SIZE)
    steps = n * n_pages // PAGES_PER_STEP
    sums = pl.pallas_call(
        functools.partial(_block_sums_kernel, layer=layer),
        grid_spec=pltpu.PrefetchScalarGridSpec(
            num_scalar_prefetch=1,
            grid=(steps,),
            in_specs=[pl.BlockSpec(memory_space=pl.ANY)],
            out_specs=pl.BlockSpec((None, blocks_per_step, MOBA_HEADS, MOBA_DH), lambda i, pt: (i, 0, 0, 0)),
            scratch_shapes=[pltpu.VMEM((2, PAGES_PER_STEP, PAGE_SIZE, MOBA_HEADS, MOBA_DH), F32),
                            pltpu.SemaphoreType.DMA((2,))],
        ),
        out_shape=jax.ShapeDtypeStruct((steps, blocks_per_step, MOBA_HEADS, MOBA_DH), F32),
        compiler_params=_cparams(1),
        name="block_sums",
    )(page_table.reshape(-1), cache)
    return sums.reshape(n, n_pages * PAGE_SIZE // MOBA_BLOCK, MOBA_HEADS, MOBA_DH)


def _select_kernel(q_ref, sums_ref, out_ref):
    nbp = sums_ref.shape[0]
    prod = (sums_ref[...] / MOBA_BLOCK) * q_ref[...]
    p_hi, p_mid, p_lo = _split3(prod.reshape(nbp * MOBA_HEADS, MOBA_DH))
    ones = jnp.ones((MOBA_DH, LANE), BF16)
    scores = _dot(p_hi, ones) + (_dot(p_mid, ones) + _dot(p_lo, ones))
    _, picks = _select_topk_rows(scores.reshape(nbp, MOBA_HEADS, LANE), True)
    for r, idx in enumerate(picks):
        out_ref[r] = idx[0]


def _select_blocks(q_heads, sums):
    n, nbp = sums.shape[:2]
    return pl.pallas_call(
        _select_kernel,
        grid=(n,),
        in_specs=[pl.BlockSpec((None, MOBA_HEADS, MOBA_DH), lambda s: (s, 0, 0)),
                  pl.BlockSpec((None, nbp, MOBA_HEADS, MOBA_DH), lambda s: (s, 0, 0, 0))],
        out_specs=pl.BlockSpec((None, MOBA_TOPK, MOBA_HEADS, LANE), lambda s: (s, 0, 0, 0)),
        out_shape=jax.ShapeDtypeStruct((n, MOBA_TOPK, MOBA_HEADS, LANE), jnp.int32),
        compiler_params=_cparams(1),
        name="select_blocks",
    )(q_heads, sums)


def _moba_decode_kernel(pages_ref, selblk_ref, bias_s_ref, q_ref, kn_ref, vn_ref, blast_ref,
                        ck_ref, cv_ref, o_ref, kbuf, vbuf, sems, *, layer, newest_block):
    pages_per_block = MOBA_BLOCK // PAGE_SIZE
    n_sel = MOBA_TOPK * pages_per_block
    seq = pl.program_id(0)
    slot = seq % 2

    def tile_copies(which_seq, which_slot):
        for h in range(MOBA_HEADS):
            for r in range(n_sel):
                page = pages_ref[(which_seq * MOBA_HEADS + h) * n_sel + r]
                yield pltpu.make_async_copy(ck_ref.at[layer, page, :, h, :], kbuf.at[which_slot, h, r],
                                            sems.at[which_slot])
                yield pltpu.make_async_copy(cv_ref.at[layer, page, :, h, :], vbuf.at[which_slot, h, r],
                                            sems.at[which_slot])

    @pl.when(seq == 0)
    def _():
        for cp in tile_copies(seq, slot):
            cp.start()

    @pl.when(seq + 1 < pl.num_programs(0))
    def _():
        for cp in tile_copies(seq + 1, 1 - slot):
            cp.start()

    for cp in tile_copies(seq, slot):
        cp.wait()

    for h in range(MOBA_HEADS):
        q8 = jnp.broadcast_to(q_ref[h:h + 1, :] * (MOBA_DH ** -0.5), (SUBLANE, MOBA_DH))
        q8b = q8.astype(BF16)
        logits = []
        for r in range(MOBA_TOPK):
            is_newest = selblk_ref[(seq * MOBA_HEADS + h) * MOBA_TOPK + r] == newest_block
            for u in range(pages_per_block):
                kp = kbuf[slot, h, r * pages_per_block + u].astype(BF16)
                s = lax.dot_general(q8b, kp, (((1,), (1,)), ((), ())), preferred_element_type=F32)
                near = blast_ref[h, :, u * PAGE_SIZE:(u + 1) * PAGE_SIZE]
                logits.append(s + jnp.where(is_newest, near, bias_s_ref[0, h]))
        s_own = jnp.sum(q8 * kn_ref[h:h + 1, :], axis=-1, keepdims=True) + bias_s_ref[1, h]
        m = s_own
        for s in logits:
            m = jnp.maximum(m, jnp.max(s, axis=-1, keepdims=True))
        p_own = jnp.exp(s_own - m)
        l = p_own
        out = p_own * vn_ref[h:h + 1, :]
        for r, s in enumerate(logits):
            p = jnp.exp(s - m)
            l = l + jnp.sum(p, axis=-1, keepdims=True)
            out = out + _dot(p.astype(BF16), vbuf[slot, h, r].astype(BF16))
        o_ref[h:h + 1, :] = (out / l)[:1]


def _moba_decode(pages, selblk, bias_s, q_heads, kn_heads, vn_heads, blast, cache_k, cache_v, layer,
                 newest_block):
    n = q_heads.shape[0]
    n_sel = MOBA_TOPK * (MOBA_BLOCK // PAGE_SIZE)
    tok = pl.BlockSpec((None, MOBA_HEADS, MOBA_DH), lambda s, pg, sb: (s, 0, 0))
    tiles = pltpu.VMEM((2, MOBA_HEADS, n_sel, PAGE_SIZE, MOBA_DH), F32)
    return pl.pallas_call(
        functools.partial(_moba_decode_kernel, layer=layer, newest_block=newest_block),
        grid_spec=pltpu.PrefetchScalarGridSpec(
            num_scalar_prefetch=2,
            grid=(n,),
            in_specs=[pl.BlockSpec(memory_space=pltpu.SMEM), tok, tok, tok, _resident(blast.shape),
                      pl.BlockSpec(memory_space=pl.ANY), pl.BlockSpec(memory_space=pl.ANY)],
            out_specs=tok,
            scratch_shapes=[tiles, tiles, pltpu.SemaphoreType.DMA((2,))],
        ),
        out_shape=jax.ShapeDtypeStruct((n, MOBA_HEADS, MOBA_DH), F32),
        compiler_params=_cparams(1),
        name="moba_decode",
    )(pages, selblk, bias_s, q_heads, kn_heads, vn_heads, blast, cache_k, cache_v)


def _rel_bucket_table(n):
    d = np.arange(n)
    max_exact = REL_BUCKETS // 2
    big = max_exact + (np.log(np.maximum(d, max_exact).astype(np.float32) / max_exact)
                       / math.log(REL_MAX_DIST / max_exact) * (REL_BUCKETS - max_exact)).astype(np.int32)
    return np.where(d < max_exact, d, np.minimum(big, REL_BUCKETS - 1)).astype(np.int32)


def _bias_tables(rel_bias):
    bucket = _rel_bucket_table(2 * MOBA_BLOCK)
    assert bucket[MOBA_BLOCK + 1:].min() == REL_BUCKETS - 1
    by_dist = rel_bias[bucket, :].T
    key = np.arange(MOBA_BLOCK)[:, None]
    qry = np.arange(MOBA_BLOCK)[None, :]
    near_t = by_dist[:, MOBA_BLOCK + qry - key]
    diag_t = jnp.where(qry >= key, by_dist[:, np.abs(qry - key)], NEG)
    far = rel_bias[REL_BUCKETS - 1, :]
    last_blk = by_dist[:, MOBA_BLOCK - np.arange(MOBA_BLOCK)]
    return near_t, diag_t, far, last_blk.reshape(MOBA_HEADS, 1, MOBA_BLOCK), rel_bias[0, :]


def kernel(x_prompt, x_sample, cache_k, cache_v, state_gla, page_table, w_in, w_alpha2, b_alpha,
           g_gla_norm, rel_bias, w_out, ln1_g, ln1_b, w_ffn_in, w_ffn_out, ln2_g, ln2_b):
    layer = 0
    n_batch, t, _ = x_prompt.shape
    n_seq, t_new, _ = x_sample.shape
    n_pages = page_table.shape[1]
    past = n_pages * PAGE_SIZE
    assert n_batch == 1 and t_new == 1 and t % MOBA_BLOCK == 0 and t % GLA_STEP == 0
    assert past % MOBA_BLOCK == 0 and past // MOBA_BLOCK >= MOBA_TOPK
    assert n_pages % PAGES_PER_STEP == 0 and n_seq % GLA_DECODE_SEQS == 0
    nbp = past // MOBA_BLOCK

    hk = GLA_HEADS * GLA_DK
    lr0 = 2 * hk + 2 * D_MODEL
    w = w_in[layer]
    w_main = jnp.concatenate([w[:, :lr0], w[:, lr0 + GLA_LOWRANK:]], axis=1).astype(BF16)
    w_lr = jnp.pad(w[:, lr0:lr0 + GLA_LOWRANK], ((0, 0), (0, LANE - GLA_LOWRANK))).astype(BF16)
    wa2 = jnp.pad(w_alpha2[layer], ((0, LANE - GLA_LOWRANK), (0, 0))).astype(BF16)
    ba = b_alpha[layer].reshape(1, hk)
    g = g_gla_norm[layer].reshape(1, GLA_DV)
    wo = w_out[layer].astype(BF16)
    w1 = w_ffn_in[layer].astype(BF16)
    w2 = w_ffn_out[layer].astype(BF16)
    ln = [a[layer].reshape(1, D_MODEL) for a in (ln1_g, ln1_b, ln2_g, ln2_b)]
    near_t, diag_t, bias_far, bias_last, bias_own = _bias_tables(rel_bias)

    xp = x_prompt.reshape(t, D_MODEL)
    qk_a, v_a, r_a, q_b, k_b, v_b, g_a, g_b, lr = _in_proj(xp, w_main, w_lr)
    o_a, s_prompt = _gla_prompt(qk_a, v_a, r_a, lr, wa2, ba, g)
    k16, vt16, means = _moba_prep(k_b, v_b)
    o_b = _moba_prompt(q_b, k16, vt16, means, near_t, diag_t, bias_far)
    y_prompt = _out_ffn(xp, o_a, o_b, g_a, g_b, wo, w1, w2, *ln)

    xs = x_sample.reshape(n_seq, D_MODEL)
    qk_s, v_s, r_s, qb_s, kb_s, vb_s, ga_s, gb_s, lr_s = _in_proj(xs, w_main, w_lr)
    oa_s, s_sample = _gla_decode(qk_s, v_s, r_s, lr_s, wa2, ba, g, state_gla[layer])
    sums = _block_sums(cache_k, layer, page_table)
    heads = lambda a: a.reshape(n_seq, MOBA_HEADS, MOBA_DH)
    picks = _select_blocks(heads(qb_s), sums)[..., 0]
    picks = picks.transpose(0, 2, 1)
    ppb = MOBA_BLOCK // PAGE_SIZE
    logical = (picks[..., None] * ppb + jnp.arange(ppb)).reshape(n_seq, -1)
    pages = jnp.take_along_axis(page_table, logical, axis=1).reshape(-1)
    bias_s = jnp.stack([bias_far, bias_own])
    ob_s = _moba_decode(pages, picks.reshape(-1), bias_s, heads(qb_s), heads(kb_s), heads(vb_s), bias_last,
                        cache_k, cache_v, layer, nbp - 1)
    y_sample = _out_ffn(xs, oa_s, ob_s.reshape(n_seq, D_MODEL), ga_s, gb_s, wo, w1, w2, *ln)

    kv_shape = lambda n, tt: (DEPTH, n, tt, MOBA_HEADS, MOBA_DH)
    return (y_prompt.reshape(1, t, D_MODEL), y_sample.reshape(n_seq, 1, D_MODEL),
            k_b.reshape(kv_shape(1, t)), v_b.reshape(kv_shape(1, t)),
            s_prompt.reshape(DEPTH, 1, GLA_HEADS, GLA_DK, GLA_DV),
            kb_s.reshape(kv_shape(n_seq, 1)), vb_s.reshape(kv_shape(n_seq, 1)),
            s_sample.reshape(DEPTH, n_seq, GLA_HEADS, GLA_DK, GLA_DV))
```

```python
import functools
import math

import jax
import jax.numpy as jnp
import numpy as np
from jax import lax
from jax.experimental import pallas as pl
from jax.experimental.pallas import tpu as pltpu

D_MODEL = 1024
DEPTH = 1
GLA_HEADS = 4
GLA_DK = 128
GLA_DV = 256
GLA_LOWRANK = 16
GLA_TAU = 16.0
GLA_CHUNK = 64
MOBA_HEADS = 8
MOBA_DH = 128
MOBA_BLOCK = 256
MOBA_TOPK = 3
PAGE_SIZE = 128
REL_BUCKETS = 32
REL_MAX_DIST = 128
DN_ALPHA = (2 * DEPTH) ** 0.25
LN_EPS = 1e-5

LANE = 128
SUBLANE = 8
VMEM_LIMIT_BYTES = 56 * 1024 * 1024

NEG = -1e30
BIG = 1e30

ROW_TILE = 256
GLA_STEP = 256
GLA_DECODE_SEQS = 8
PAGES_PER_STEP = 16
FAR_BLOCKS = 8
VT_ROWS = 128 + 16

LOG2E = math.log2(math.e)

F32 = jnp.float32
BF16 = jnp.bfloat16


def _cparams(n_axes):
    return pltpu.CompilerParams(dimension_semantics=("arbitrary",) * n_axes,
                                vmem_limit_bytes=VMEM_LIMIT_BYTES)


def _resident(shape):
    return pl.BlockSpec(shape, lambda *_: (0,) * len(shape), pipeline_mode=pl.Buffered(1))


def _split3(a):
    hi = a.astype(BF16)
    r = a - hi.astype(F32)
    mid = r.astype(BF16)
    lo = (r - mid.astype(F32)).astype(BF16)
    return hi, mid, lo


def _dot(a, b):
    return jnp.dot(a, b, preferred_element_type=F32)


def _dot_f32ish(a, b):
    ah, al, _ = _split3(a)
    bh, bl, _ = _split3(b)
    return _dot(ah, bh) + (_dot(ah, bl) + _dot(al, bh))


def _log_sigmoid(z):
    return jnp.minimum(z, 0.0) - jnp.log1p(jnp.exp(-jnp.abs(z)))


def _silu(x):
    return x * jax.nn.sigmoid(x)


def _layer_norm(x, g, b):
    mu = jnp.mean(x, -1, keepdims=True)
    xc = x - mu
    var = jnp.mean(xc * xc, -1, keepdims=True)
    return xc * lax.rsqrt(var + LN_EPS) * g + b


def _in_proj_kernel(x_ref, w_ref, wlr_ref, *out_refs):
    xb = x_ref[...].astype(BF16)
    for g, o_ref in enumerate(out_refs[:-1]):
        o_ref[...] = _dot(xb, w_ref[:, g * D_MODEL:(g + 1) * D_MODEL])
    out_refs[-1][...] = _dot(xb, wlr_ref[...])


def _in_proj(x2d, w_main, w_lr):
    rows = x2d.shape[0]
    tm = min(ROW_TILE, rows)
    groups = w_main.shape[1] // D_MODEL
    row_block = lambda width: pl.BlockSpec((tm, width), lambda i: (i, 0))
    return pl.pallas_call(
        _in_proj_kernel,
        grid=(rows // tm,),
        in_specs=[row_block(D_MODEL), _resident(w_main.shape), _resident(w_lr.shape)],
        out_specs=[row_block(D_MODEL)] * groups + [row_block(LANE)],
        out_shape=[jax.ShapeDtypeStruct((rows, D_MODEL), F32)] * groups
        + [jax.ShapeDtypeStruct((rows, LANE), F32)],
        compiler_params=_cparams(1),
        name="in_proj",
    )(x2d, w_main, w_lr)


def _out_ffn_kernel(x_ref, oa_ref, ob_ref, ga_ref, gb_ref, wo_ref, w1_ref, w2_ref,
                    ln1g_ref, ln1b_ref, ln2g_ref, ln2b_ref, y_ref):
    d_ff = w2_ref.shape[0]
    merged = jax.nn.sigmoid(ga_ref[...]) * oa_ref[...] + jax.nn.sigmoid(gb_ref[...]) * ob_ref[...]
    mix = _dot(merged.astype(BF16), wo_ref[...])
    x1 = _layer_norm(DN_ALPHA * x_ref[...] + mix, ln1g_ref[...], ln1b_ref[...])
    hidden = _dot(x1.astype(BF16), w1_ref[...])
    act = _silu(hidden[:, :d_ff]) * hidden[:, d_ff:]
    ffn = _dot(act.astype(BF16), w2_ref[...])
    y_ref[...] = _layer_norm(DN_ALPHA * x1 + ffn, ln2g_ref[...], ln2b_ref[...])


def _out_ffn(x2d, o_a, o_b, g_a, g_b, wo, w1, w2, ln1g, ln1b, ln2g, ln2b):
    rows = x2d.shape[0]
    tm = min(ROW_TILE, rows)
    row_block = pl.BlockSpec((tm, D_MODEL), lambda i: (i, 0))
    vec = _resident((1, D_MODEL))
    return pl.pallas_call(
        _out_ffn_kernel,
        grid=(rows // tm,),
        in_specs=[row_block] * 5 + [_resident(wo.shape), _resident(w1.shape), _resident(w2.shape)]
        + [vec] * 4,
        out_specs=row_block,
        out_shape=jax.ShapeDtypeStruct((rows, D_MODEL), F32),
        compiler_params=_cparams(1),
        name="out_ffn",
    )(x2d, o_a, o_b, g_a, g_b, wo, w1, w2, ln1g, ln1b, ln2g, ln2b)


def _gla_gates(lr, wa2_ref, ba_ref):
    z = _dot(lr.astype(BF16), wa2_ref[...]) + ba_ref[...]
    return _log_sigmoid(z) / GLA_TAU


def _gla_out_norm(o, g, r):
    o = o * lax.rsqrt(jnp.mean(o * o, -1, keepdims=True) + LN_EPS) * g
    return o * _silu(r)


def _gla_prompt_kernel(qk_ref, v_ref, r_ref, lr_ref, wa2_ref, ba_ref, g_ref, o_ref, s_ref):
    c = GLA_CHUNK
    hk = GLA_HEADS * GLA_DK

    @pl.when(pl.program_id(0) == 0)
    def _():
        s_ref[...] = jnp.zeros(s_ref.shape, F32)

    row = lax.broadcasted_iota(jnp.int32, (c, c), 0)
    col = lax.broadcasted_iota(jnp.int32, (c, c), 1)
    causal = row >= col
    tri = jnp.where(causal, 1.0, 0.0).astype(BF16)
    for ci in range(GLA_STEP // c):
        rows = slice(ci * c, (ci + 1) * c)
        la = _gla_gates(lr_ref[rows, :], wa2_ref, ba_ref)
        la_hi, la_mid, la_lo = _split3(la)
        b = _dot(tri, la_hi) + (_dot(tri, la_mid) + _dot(tri, la_lo))
        for h in range(GLA_HEADS):
            kcols = slice(h * GLA_DK, (h + 1) * GLA_DK)
            vcols = slice(h * GLA_DV, (h + 1) * GLA_DV)
            bh = b[:, kcols]
            qh = qk_ref[rows, kcols] * (GLA_DK ** -0.5)
            kh = qk_ref[rows, hk + h * GLA_DK: hk + (h + 1) * GLA_DK]
            vh = v_ref[rows, vcols].astype(BF16)
            s_old = s_ref[h]
            qt = (qh * jnp.exp(bh)).astype(BF16)
            b_t = bh.T
            k_t = kh.T
            kt_t = (k_t * jnp.exp(-b_t)).astype(BF16)
            att = jnp.where(causal, _dot(qt, kt_t), 0.0)
            o = _dot(qt, s_old.astype(BF16)) + _dot(att.astype(BF16), vh)
            b_last = b_t[:, c - 1:c]
            kd_t = (k_t * jnp.exp(b_last - b_t)).astype(BF16)
            s_ref[h] = s_old * jnp.exp(b_last) + _dot(kd_t, vh)
            o_ref[rows, vcols] = _gla_out_norm(o, g_ref[...], r_ref[rows, vcols])


def _gla_prompt(qk, v, r, lr, wa2, ba, g):
    t = qk.shape[0]
    step = lambda width: pl.BlockSpec((GLA_STEP, width), lambda i: (i, 0))
    state_shape = (GLA_HEADS, GLA_DK, GLA_DV)
    return pl.pallas_call(
        _gla_prompt_kernel,
        grid=(t // GLA_STEP,),
        in_specs=[step(D_MODEL), step(D_MODEL), step(D_MODEL), step(LANE),
                  _resident(wa2.shape), _resident(ba.shape), _resident(g.shape)],
        out_specs=[step(D_MODEL), pl.BlockSpec(state_shape, lambda i: (0, 0, 0))],
        out_shape=[jax.ShapeDtypeStruct((t, D_MODEL), F32), jax.ShapeDtypeStruct(state_shape, F32)],
        compiler_params=_cparams(1),
        name="gla_prompt",
    )(qk, v, r, lr, wa2, ba, g)


def _gla_decode_kernel(qk_ref, v_ref, r_ref, lr_ref, wa2_ref, ba_ref, g_ref, s0_ref,
                       o_ref, s_ref, o_scr):
    hk = GLA_HEADS * GLA_DK
    a = jnp.exp(_gla_gates(lr_ref[...], wa2_ref, ba_ref))
    for h in range(GLA_HEADS):
        kcols = slice(h * GLA_DK, (h + 1) * GLA_DK)
        vcols = slice(h * GLA_DV, (h + 1) * GLA_DV)
        a_t = a[:, kcols].T
        q_t = (qk_ref[:, kcols] * (GLA_DK ** -0.5)).T
        k_t = qk_ref[:, hk + h * GLA_DK: hk + (h + 1) * GLA_DK].T
        for n in range(GLA_DECODE_SEQS):
            s_new = s0_ref[n, h] * a_t[:, n:n + 1] + k_t[:, n:n + 1] * v_ref[n:n + 1, vcols]
            s_ref[n, h] = s_new
            o_scr[n:n + 1, vcols] = jnp.sum(q_t[:, n:n + 1] * s_new, axis=0, keepdims=True)
    for h in range(GLA_HEADS):
        vcols = slice(h * GLA_DV, (h + 1) * GLA_DV)
        o_ref[:, vcols] = _gla_out_norm(o_scr[:, vcols], g_ref[...], r_ref[:, vcols])


def _gla_decode(qk, v, r, lr, wa2, ba, g, s0):
    n = qk.shape[0]
    nb = GLA_DECODE_SEQS
    rows = lambda width: pl.BlockSpec((nb, width), lambda i: (i, 0))
    state = pl.BlockSpec((nb, GLA_HEADS, GLA_DK, GLA_DV), lambda i: (i, 0, 0, 0))
    return pl.pallas_call(
        _gla_decode_kernel,
        grid=(n // nb,),
        in_specs=[rows(D_MODEL), rows(D_MODEL), rows(D_MODEL), rows(LANE),
                  _resident(wa2.shape), _resident(ba.shape), _resident(g.shape), state],
        out_specs=[rows(D_MODEL), state],
        out_shape=[jax.ShapeDtypeStruct((n, D_MODEL), F32), jax.ShapeDtypeStruct(s0.shape, F32)],
        scratch_shapes=[pltpu.VMEM((nb, D_MODEL), F32)],
        compiler_params=_cparams(1),
        name="gla_decode",
    )(qk, v, r, lr, wa2, ba, g, s0)


def _moba_prep_kernel(k_ref, v_ref, k16_ref, vt16_ref, mean_ref):
    k = k_ref[...]
    k16_ref[...] = k.astype(BF16)
    mean_ref[...] = jnp.sum(k, axis=0, keepdims=True) / MOBA_BLOCK
    v_t = v_ref[...].T.astype(BF16)
    for h in range(MOBA_HEADS):
        vt16_ref[h, :MOBA_DH, :] = v_t[h * MOBA_DH:(h + 1) * MOBA_DH]
    vt16_ref[:, MOBA_DH:, :] = jnp.ones((MOBA_HEADS, VT_ROWS - MOBA_DH, MOBA_BLOCK), BF16)


def _moba_prep(k, v):
    t = k.shape[0]
    nb = t // MOBA_BLOCK
    blk = pl.BlockSpec((MOBA_BLOCK, D_MODEL), lambda i: (i, 0))
    k16, vt16, means = pl.pallas_call(
        _moba_prep_kernel,
        grid=(nb,),
        in_specs=[blk, blk],
        out_specs=[blk, pl.BlockSpec((MOBA_HEADS, VT_ROWS, MOBA_BLOCK), lambda i: (0, 0, i)),
                   pl.BlockSpec((None, 1, D_MODEL), lambda i: (i, 0, 0))],
        out_shape=[jax.ShapeDtypeStruct((t, D_MODEL), BF16),
                   jax.ShapeDtypeStruct((MOBA_HEADS, VT_ROWS, t), BF16),
                   jax.ShapeDtypeStruct((nb, 1, D_MODEL), F32)],
        compiler_params=_cparams(1),
        name="moba_prep",
    )(k, v)
    return k16, vt16, means.reshape(nb, D_MODEL)


def _select_topk_rows(scores, valid):
    n = scores.shape[0]
    blk = lax.broadcasted_iota(jnp.int32, scores.shape, 0)
    sc = jnp.where(valid, scores, -jnp.inf)
    sel = jnp.zeros(scores.shape, F32)
    picks = []
    for _ in range(MOBA_TOPK):
        mx = jnp.max(sc, axis=0, keepdims=True)
        idx = jnp.min(jnp.where(sc == mx, blk, n), axis=0, keepdims=True)
        hit = blk == idx
        sel = jnp.where(jnp.logical_and(hit, valid), 1.0, sel)
        sc = jnp.where(hit, -jnp.inf, sc)
        picks.append(idx)
    return sel, picks


def _query_block(step, nb):
    half = lax.shift_right_logical(step, 1)
    return jnp.where(jnp.bitwise_and(step, 1) == 0, half, nb - 1 - half)


def _page_copies(pt_ref, cache_ref, buf, sems, layer, step, slot):
    for u in range(PAGES_PER_STEP):
        page = pt_ref[step * PAGES_PER_STEP + u]
        yield pltpu.make_async_copy(cache_ref.at[layer, page], buf.at[slot, u], sems.at[slot])


def _moba_prompt_kernel(pt_ref, cfar_ref, q_ref, k_ref, vt_ref, mean_ref, bnear_ref, bdiag_ref, cache_ref,
                        o_ref, sums_ref, sel_ref, acc_ref, s0_ref, s1_ref, snd_ref, page_buf, page_sems,
                        *, layer, n_sum_steps):
    h = pl.program_id(0)
    nb = sel_ref.shape[0]
    i = _query_block(pl.program_id(1), nb)
    blk_q = MOBA_BLOCK
    step = h * nb + pl.program_id(1)

    @pl.when(step == 0)
    def _():
        for cp in _page_copies(pt_ref, cache_ref, page_buf, page_sems, layer, step, 0):
            cp.start()

    @pl.when(step + 1 < n_sum_steps)
    def _():
        for cp in _page_copies(pt_ref, cache_ref, page_buf, page_sems, layer, step + 1, lax.rem(step + 1, 2)):
            cp.start()
    q_t = q_ref[...].T
    scores = _dot_f32ish(mean_ref[...], q_t)
    blk = lax.broadcasted_iota(jnp.int32, scores.shape, 0)
    sel, _ = _select_topk_rows(scores, blk < i)
    sel_ref[...] = sel
    qs = (q_t * (MOBA_DH ** -0.5 * LOG2E)).astype(BF16)
    acc_ref[...] = jnp.zeros(acc_ref.shape, F32)
    n_far = jnp.maximum(i - 1, 0)
    cfar = cfar_ref[h]

    def block_off(j):
        return pl.multiple_of(j * MOBA_BLOCK, MOBA_BLOCK)

    def logits(j):
        return _dot(k_ref[pl.ds(block_off(j), MOBA_BLOCK), :], qs)

    def absorb(m_old, j, s, selected, shift_const):
        cm = jnp.max(s, axis=0, keepdims=True) + shift_const
        m_new = jnp.where(selected, jnp.maximum(m_old, cm), m_old)
        alpha = jnp.exp2(m_old - m_new)
        p = jnp.exp2(s - jnp.where(selected, m_new - shift_const, BIG))
        pv = _dot(vt_ref[:, pl.ds(block_off(j), MOBA_BLOCK)], p.astype(BF16))
        acc_ref[...] = alpha * acc_ref[...] + pv
        return m_new

    def far_logits_into(dst_ref, trip):
        j0 = jnp.minimum(trip * FAR_BLOCKS, nb - FAR_BLOCKS)
        for g in range(FAR_BLOCKS):
            dst_ref[g * MOBA_BLOCK:(g + 1) * MOBA_BLOCK, :] = logits(j0 + g)

    def far_trip(trip, m, src_ref, dst_ref):
        far_logits_into(dst_ref, trip + 1)
        for g in range(FAR_BLOCKS):
            j = trip * FAR_BLOCKS + g
            sel_g = jnp.logical_and(sel_ref[pl.ds(j, 1), :] > 0.5, j < n_far)
            m = absorb(m, j, src_ref[g * MOBA_BLOCK:(g + 1) * MOBA_BLOCK, :], sel_g, cfar)
        return m

    def far_body(trip, m):
        return lax.cond(lax.rem(trip, 2) == 0,
                        lambda mm: far_trip(trip, mm, s0_ref, s1_ref),
                        lambda mm: far_trip(trip, mm, s1_ref, s0_ref), m)

    far_logits_into(s0_ref, 0)
    snd_ref[:MOBA_BLOCK, :] = logits(n_far) + bnear_ref[...]
    snd_ref[MOBA_BLOCK:, :] = logits(i) + bdiag_ref[...]
    m = jnp.full((1, blk_q), NEG, F32)
    m = lax.fori_loop(0, lax.div(n_far + (FAR_BLOCKS - 1), FAR_BLOCKS), far_body, m)

    def reduce_pages():
        slot = lax.rem(step, 2)
        for cp in _page_copies(pt_ref, cache_ref, page_buf, page_sems, layer, step, slot):
            cp.wait()
        pages_per_block = MOBA_BLOCK // PAGE_SIZE
        for b in range(PAGES_PER_STEP // pages_per_block):
            total = jnp.sum(page_buf[slot, b * pages_per_block], axis=0)
            for u in range(1, pages_per_block):
                total = total + jnp.sum(page_buf[slot, b * pages_per_block + u], axis=0)
            sums_ref[b] = total

    every_step = n_sum_steps == MOBA_HEADS * nb
    if every_step:
        reduce_pages()
    near_sel = jnp.logical_and(sel_ref[pl.ds(n_far, 1), :] > 0.5, i >= 1)
    m = absorb(m, n_far, snd_ref[:MOBA_BLOCK, :], near_sel, 0.0)
    absorb(m, i, snd_ref[MOBA_BLOCK:, :], True, 0.0)
    o_ref[...] = (acc_ref[:MOBA_DH, :] / acc_ref[MOBA_DH:MOBA_DH + 1, :]).T
    if not every_step:
        pl.when(step < n_sum_steps)(reduce_pages)


def _moba_prompt(q, k16, vt16, means, bnear_t, bdiag_t, cfar, cache_k, layer, page_table):
    t = q.shape[0]
    nb = t // MOBA_BLOCK
    n_seq, n_pages = page_table.shape
    n_sum_steps = n_seq * n_pages // PAGES_PER_STEP
    blocks_per_step = PAGES_PER_STEP // (MOBA_BLOCK // PAGE_SIZE)
    assert nb % FAR_BLOCKS == 0 and n_sum_steps <= MOBA_HEADS * nb
    bias_spec = pl.BlockSpec((None, MOBA_BLOCK, MOBA_BLOCK), lambda h, s, pt: (h, 0, 0))
    tile_spec = pl.BlockSpec((MOBA_BLOCK, MOBA_DH), lambda h, s, pt: (_query_block(s, nb), h))
    trip_logits = pltpu.VMEM((FAR_BLOCKS * MOBA_BLOCK, MOBA_BLOCK), F32)
    o_b, sums = pl.pallas_call(
        functools.partial(_moba_prompt_kernel, layer=layer, n_sum_steps=n_sum_steps),
        grid_spec=pltpu.PrefetchScalarGridSpec(
            num_scalar_prefetch=1,
            grid=(MOBA_HEADS, nb),
            in_specs=[pl.BlockSpec(memory_space=pltpu.SMEM),
                      tile_spec,
                      pl.BlockSpec((t, MOBA_DH), lambda h, s, pt: (0, h)),
                      pl.BlockSpec((None, VT_ROWS, t), lambda h, s, pt: (h, 0, 0)),
                      pl.BlockSpec((nb, MOBA_DH), lambda h, s, pt: (0, h)),
                      bias_spec, bias_spec,
                      pl.BlockSpec(memory_space=pl.ANY)],
            out_specs=[tile_spec,
                       pl.BlockSpec((None, blocks_per_step, MOBA_HEADS, MOBA_DH),
                                    lambda h, s, pt: (jnp.minimum(h * nb + s, n_sum_steps - 1), 0, 0, 0))],
            scratch_shapes=[pltpu.VMEM((nb, MOBA_BLOCK), F32), pltpu.VMEM((VT_ROWS, MOBA_BLOCK), F32),
                            trip_logits, trip_logits, pltpu.VMEM((2 * MOBA_BLOCK, MOBA_BLOCK), F32),
                            pltpu.VMEM((2, PAGES_PER_STEP, PAGE_SIZE, MOBA_HEADS, MOBA_DH), F32),
                            pltpu.SemaphoreType.DMA((2,))],
        ),
        out_shape=[jax.ShapeDtypeStruct((t, D_MODEL), F32),
                   jax.ShapeDtypeStruct((n_sum_steps, blocks_per_step, MOBA_HEADS, MOBA_DH), F32)],
        compiler_params=_cparams(2),
        name="moba_prompt",
    )(page_table.reshape(-1), cfar, q, k16, vt16, means, bnear_t, bdiag_t, cache_k)
    return o_b, sums.reshape(n_seq, n_pages * PAGE_SIZE // MOBA_BLOCK, MOBA_HEADS, MOBA_DH)


def _select_kernel(q_ref, sums_ref, out_ref):
    nbp = sums_ref.shape[0]
    prod = (sums_ref[...] / MOBA_BLOCK) * q_ref[...]
    p_hi, p_mid, p_lo = _split3(prod.reshape(nbp * MOBA_HEADS, MOBA_DH))
    ones = jnp.ones((MOBA_DH, LANE), BF16)
    scores = _dot(p_hi, ones) + (_dot(p_mid, ones) + _dot(p_lo, ones))
    _, picks = _select_topk_rows(scores.reshape(nbp, MOBA_HEADS, LANE), True)
    for r, idx in enumerate(picks):
        out_ref[r] = idx[0]


def _select_blocks(q_heads, sums):
    n, nbp = sums.shape[:2]
    return pl.pallas_call(
        _select_kernel,
        grid=(n,),
        in_specs=[pl.BlockSpec((None, MOBA_HEADS, MOBA_DH), lambda s: (s, 0, 0)),
                  pl.BlockSpec((None, nbp, MOBA_HEADS, MOBA_DH), lambda s: (s, 0, 0, 0))],
        out_specs=pl.BlockSpec((None, MOBA_TOPK, MOBA_HEADS, LANE), lambda s: (s, 0, 0, 0)),
        out_shape=jax.ShapeDtypeStruct((n, MOBA_TOPK, MOBA_HEADS, LANE), jnp.int32),
        compiler_params=_cparams(1),
        name="select_blocks",
    )(q_heads, sums)


def _moba_decode_kernel(pages_ref, selblk_ref, bias_s_ref, q_ref, kn_ref, vn_ref, blast_ref,
                        ck_ref, cv_ref, o_ref, kbuf, vbuf, sems, *, layer, newest_block):
    pages_per_block = MOBA_BLOCK // PAGE_SIZE
    n_sel = MOBA_TOPK * pages_per_block
    seq = pl.program_id(0)
    slot = seq % 2

    def tile_copies(which_seq, which_slot):
        for h in range(MOBA_HEADS):
            for r in range(n_sel):
                page = pages_ref[(which_seq * MOBA_HEADS + h) * n_sel + r]
                yield pltpu.make_async_copy(ck_ref.at[layer, page, :, h, :], kbuf.at[which_slot, h, r],
                                            sems.at[which_slot])
                yield pltpu.make_async_copy(cv_ref.at[layer, page, :, h, :], vbuf.at[which_slot, h, r],
                                            sems.at[which_slot])

    @pl.when(seq == 0)
    def _():
        for cp in tile_copies(seq, slot):
            cp.start()

    @pl.when(seq + 1 < pl.num_programs(0))
    def _():
        for cp in tile_copies(seq + 1, 1 - slot):
            cp.start()

    for cp in tile_copies(seq, slot):
        cp.wait()

    for h in range(MOBA_HEADS):
        q8 = jnp.broadcast_to(q_ref[h:h + 1, :] * (MOBA_DH ** -0.5), (SUBLANE, MOBA_DH))
        q8b = q8.astype(BF16)
        logits = []
        for r in range(MOBA_TOPK):
            is_newest = selblk_ref[(seq * MOBA_HEADS + h) * MOBA_TOPK + r] == newest_block
            for u in range(pages_per_block):
                kp = kbuf[slot, h, r * pages_per_block + u].astype(BF16)
                s = lax.dot_general(q8b, kp, (((1,), (1,)), ((), ())), preferred_element_type=F32)
                near = blast_ref[h, :, u * PAGE_SIZE:(u + 1) * PAGE_SIZE]
                logits.append(s + jnp.where(is_newest, near, bias_s_ref[0, h]))
        s_own = jnp.sum(q8 * kn_ref[h:h + 1, :], axis=-1, keepdims=True) + bias_s_ref[1, h]
        m = s_own
        for s in logits:
            m = jnp.maximum(m, jnp.max(s, axis=-1, keepdims=True))
        p_own = jnp.exp(s_own - m)
        l = p_own
        out = p_own * vn_ref[h:h + 1, :]
        for r, s in enumerate(logits):
            p = jnp.exp(s - m)
            l = l + jnp.sum(p, axis=-1, keepdims=True)
            out = out + _dot(p.astype(BF16), vbuf[slot, h, r].astype(BF16))
        o_ref[h:h + 1, :] = (out / l)[:1]


def _moba_decode(pages, selblk, bias_s, q_heads, kn_heads, vn_heads, blast, cache_k, cache_v, layer,
                 newest_block):
    n = q_heads.shape[0]
    n_sel = MOBA_TOPK * (MOBA_BLOCK // PAGE_SIZE)
    tok = pl.BlockSpec((None, MOBA_HEADS, MOBA_DH), lambda s, pg, sb: (s, 0, 0))
    tiles = pltpu.VMEM((2, MOBA_HEADS, n_sel, PAGE_SIZE, MOBA_DH), F32)
    return pl.pallas_call(
        functools.partial(_moba_decode_kernel, layer=layer, newest_block=newest_block),
        grid_spec=pltpu.PrefetchScalarGridSpec(
            num_scalar_prefetch=2,
            grid=(n,),
            in_specs=[pl.BlockSpec(memory_space=pltpu.SMEM), tok, tok, tok, _resident(blast.shape),
                      pl.BlockSpec(memory_space=pl.ANY), pl.BlockSpec(memory_space=pl.ANY)],
            out_specs=tok,
            scratch_shapes=[tiles, tiles, pltpu.SemaphoreType.DMA((2,))],
        ),
        out_shape=jax.ShapeDtypeStruct((n, MOBA_HEADS, MOBA_DH), F32),
        compiler_params=_cparams(1),
        name="moba_decode",
    )(pages, selblk, bias_s, q_heads, kn_heads, vn_heads, blast, cache_k, cache_v)


def _rel_bucket_table(n):
    d = np.arange(n)
    max_exact = REL_BUCKETS // 2
    big = max_exact + (np.log(np.maximum(d, max_exact).astype(np.float32) / max_exact)
                       / math.log(REL_MAX_DIST / max_exact) * (REL_BUCKETS - max_exact)).astype(np.int32)
    return np.where(d < max_exact, d, np.minimum(big, REL_BUCKETS - 1)).astype(np.int32)


def _bias_tables(rel_bias):
    bucket = _rel_bucket_table(2 * MOBA_BLOCK)
    assert bucket[MOBA_BLOCK + 1:].min() == REL_BUCKETS - 1
    one_hot = (bucket[:, None] == np.arange(REL_BUCKETS)[None, :]).astype(np.float32)
    by_dist = jnp.dot(one_hot, rel_bias, precision=lax.Precision.HIGHEST).T
    n = MOBA_BLOCK

    def toeplitz_t(g):
        g = jnp.pad(g, ((0, 0), (0, 1)))
        shifted = jnp.tile(g, (1, n))[:, :n * 2 * n].reshape(MOBA_HEADS, n, 2 * n)
        return shifted[:, :, n:]

    near_t = toeplitz_t(by_dist)
    diag_t = toeplitz_t(jnp.concatenate([jnp.full((MOBA_HEADS, n), NEG, F32), by_dist[:, :n]], axis=1))
    far = rel_bias[REL_BUCKETS - 1, :]
    last_blk = by_dist[:, n:0:-1]
    return near_t, diag_t, far, last_blk.reshape(MOBA_HEADS, 1, MOBA_BLOCK), rel_bias[0, :]


def kernel(x_prompt, x_sample, cache_k, cache_v, state_gla, page_table, w_in, w_alpha2, b_alpha,
           g_gla_norm, rel_bias, w_out, ln1_g, ln1_b, w_ffn_in, w_ffn_out, ln2_g, ln2_b):
    layer = 0
    n_batch, t, _ = x_prompt.shape
    n_seq, t_new, _ = x_sample.shape
    n_pages = page_table.shape[1]
    past = n_pages * PAGE_SIZE
    assert n_batch == 1 and t_new == 1 and t % MOBA_BLOCK == 0 and t % GLA_STEP == 0
    assert past % MOBA_BLOCK == 0 and past // MOBA_BLOCK >= MOBA_TOPK
    assert n_pages % PAGES_PER_STEP == 0 and n_seq % GLA_DECODE_SEQS == 0
    nbp = past // MOBA_BLOCK

    hk = GLA_HEADS * GLA_DK
    lr0 = 2 * hk + 2 * D_MODEL
    w = w_in[layer]
    w_main = jnp.concatenate([w[:, :lr0], w[:, lr0 + GLA_LOWRANK:]], axis=1).astype(BF16)
    w_lr = jnp.pad(w[:, lr0:lr0 + GLA_LOWRANK], ((0, 0), (0, LANE - GLA_LOWRANK))).astype(BF16)
    wa2 = jnp.pad(w_alpha2[layer], ((0, LANE - GLA_LOWRANK), (0, 0))).astype(BF16)
    ba = b_alpha[layer].reshape(1, hk)
    g = g_gla_norm[layer].reshape(1, GLA_DV)
    wo = w_out[layer].astype(BF16)
    w1 = w_ffn_in[layer].astype(BF16)
    w2 = w_ffn_out[layer].astype(BF16)
    ln = [a[layer].reshape(1, D_MODEL) for a in (ln1_g, ln1_b, ln2_g, ln2_b)]
    near_t, diag_t, bias_far, bias_last, bias_own = _bias_tables(rel_bias)

    xp = x_prompt.reshape(t, D_MODEL)
    qk_a, v_a, r_a, q_b, k_b, v_b, g_a, g_b, lr = _in_proj(xp, w_main, w_lr)
    o_a, s_prompt = _gla_prompt(qk_a, v_a, r_a, lr, wa2, ba, g)
    k16, vt16, means = _moba_prep(k_b, v_b)
    o_b, sums = _moba_prompt(q_b, k16, vt16, means, near_t * LOG2E, diag_t * LOG2E, bias_far * LOG2E,
                             cache_k, layer, page_table)
    y_prompt = _out_ffn(xp, o_a, o_b, g_a, g_b, wo, w1, w2, *ln)

    xs = x_sample.reshape(n_seq, D_MODEL)
    qk_s, v_s, r_s, qb_s, kb_s, vb_s, ga_s, gb_s, lr_s = _in_proj(xs, w_main, w_lr)
    oa_s, s_sample = _gla_decode(qk_s, v_s, r_s, lr_s, wa2, ba, g, state_gla[layer])
    heads = lambda a: a.reshape(n_seq, MOBA_HEADS, MOBA_DH)
    picks = _select_blocks(heads(qb_s), sums)[..., 0]
    picks = picks.transpose(0, 2, 1)
    ppb = MOBA_BLOCK // PAGE_SIZE
    logical = (picks[..., None] * ppb + jnp.arange(ppb)).reshape(n_seq, -1)
    pages = jnp.take_along_axis(page_table, logical, axis=1).reshape(-1)
    bias_s = jnp.stack([bias_far, bias_own])
    ob_s = _moba_decode(pages, picks.reshape(-1), bias_s, heads(qb_s), heads(kb_s), heads(vb_s), bias_last,
                        cache_k, cache_v, layer, nbp - 1)
    y_sample = _out_ffn(xs, oa_s, ob_s.reshape(n_seq, D_MODEL), ga_s, gb_s, wo, w1, w2, *ln)

    kv_shape = lambda n, tt: (DEPTH, n, tt, MOBA_HEADS, MOBA_DH)
    return (y_prompt.reshape(1, t, D_MODEL), y_sample.reshape(n_seq, 1, D_MODEL),
            k_b.reshape(kv_shape(1, t)), v_b.reshape(kv_shape(1, t)),
            s_prompt.reshape(DEPTH, 1, GLA_HEADS, GLA_DK, GLA_DV),
            kb_s.reshape(kv_shape(n_seq, 1)), vb_s.reshape(kv_shape(n_seq, 1)),
            s_sample.reshape(DEPTH, n_seq, GLA_HEADS, GLA_DK, GLA_DV))
```

```python
import functools
import math

import jax
import jax.numpy as jnp
import numpy as np
from jax import lax
from jax.experimental import pallas as pl
from jax.experimental.pallas import tpu as pltpu

D_MODEL = 1024
DEPTH = 1
GLA_HEADS = 4
GLA_DK = 128
GLA_DV = 256
GLA_LOWRANK = 16
GLA_TAU = 16.0
GLA_CHUNK = 64
MOBA_HEADS = 8
MOBA_DH = 128
MOBA_BLOCK = 256
MOBA_TOPK = 3
PAGE_SIZE = 128
REL_BUCKETS = 32
REL_MAX_DIST = 128
DN_ALPHA = (2 * DEPTH) ** 0.25
LN_EPS = 1e-5

LANE = 128
SUBLANE = 8
VMEM_LIMIT_BYTES = 56 * 1024 * 1024

NEG = -1e30
BIG = 1e30

ROW_TILE = 256
GLA_STEP = 256
GLA_SUBCHUNK = 16
GLA_DECODE_SEQS = 8
SELECT_SEQS = 8
PAGES_PER_STEP = 16
FAR_BLOCKS = 8
VT_ROWS = 128 + 16

LOG2E = math.log2(math.e)

F32 = jnp.float32
BF16 = jnp.bfloat16


def _cparams(n_axes):
    return pltpu.CompilerParams(dimension_semantics=("arbitrary",) * n_axes,
                                vmem_limit_bytes=VMEM_LIMIT_BYTES)


def _resident(shape):
    return pl.BlockSpec(shape, lambda *_: (0,) * len(shape), pipeline_mode=pl.Buffered(1))


def _split3(a):
    hi = a.astype(BF16)
    r = a - hi.astype(F32)
    mid = r.astype(BF16)
    lo = (r - mid.astype(F32)).astype(BF16)
    return hi, mid, lo


def _dot(a, b):
    return jnp.dot(a, b, preferred_element_type=F32)


def _dot_f32ish(a, b):
    ah, al, _ = _split3(a)
    bh, bl, _ = _split3(b)
    return _dot(ah, bh) + (_dot(ah, bl) + _dot(al, bh))


def _log_sigmoid(z):
    return jnp.minimum(z, 0.0) - jnp.log1p(jnp.exp(-jnp.abs(z)))


def _silu(x):
    return x * jax.nn.sigmoid(x)


def _layer_norm(x, g, b):
    mu = jnp.mean(x, -1, keepdims=True)
    xc = x - mu
    var = jnp.mean(xc * xc, -1, keepdims=True)
    return xc * lax.rsqrt(var + LN_EPS) * g + b


def _in_proj_kernel(x_ref, w_ref, wlr_ref, *out_refs):
    xb = x_ref[...].astype(BF16)
    for g, o_ref in enumerate(out_refs[:-1]):
        o_ref[...] = _dot(xb, w_ref[:, g * D_MODEL:(g + 1) * D_MODEL])
    out_refs[-1][...] = _dot(xb, wlr_ref[...])


def _in_proj(x2d, w_main, w_lr):
    rows = x2d.shape[0]
    tm = min(ROW_TILE, rows)
    groups = w_main.shape[1] // D_MODEL
    row_block = lambda width: pl.BlockSpec((tm, width), lambda i: (i, 0))
    return pl.pallas_call(
        _in_proj_kernel,
        grid=(rows // tm,),
        in_specs=[row_block(D_MODEL), _resident(w_main.shape), _resident(w_lr.shape)],
        out_specs=[row_block(D_MODEL)] * groups + [row_block(LANE)],
        out_shape=[jax.ShapeDtypeStruct((rows, D_MODEL), F32)] * groups
        + [jax.ShapeDtypeStruct((rows, LANE), F32)],
        compiler_params=_cparams(1),
        name="in_proj",
    )(x2d, w_main, w_lr)


def _out_ffn_kernel(x_ref, oa_ref, ob_ref, ga_ref, gb_ref, wo_ref, w1_ref, w2_ref,
                    ln1g_ref, ln1b_ref, ln2g_ref, ln2b_ref, y_ref):
    d_ff = w2_ref.shape[0]
    merged = jax.nn.sigmoid(ga_ref[...]) * oa_ref[...] + jax.nn.sigmoid(gb_ref[...]) * ob_ref[...]
    mix = _dot(merged.astype(BF16), wo_ref[...])
    x1 = _layer_norm(DN_ALPHA * x_ref[...] + mix, ln1g_ref[...], ln1b_ref[...])
    hidden = _dot(x1.astype(BF16), w1_ref[...])
    act = _silu(hidden[:, :d_ff]) * hidden[:, d_ff:]
    ffn = _dot(act.astype(BF16), w2_ref[...])
    y_ref[...] = _layer_norm(DN_ALPHA * x1 + ffn, ln2g_ref[...], ln2b_ref[...])


def _out_ffn(x2d, o_a, o_b, g_a, g_b, wo, w1, w2, ln1g, ln1b, ln2g, ln2b):
    rows = x2d.shape[0]
    tm = min(ROW_TILE, rows)
    row_block = pl.BlockSpec((tm, D_MODEL), lambda i: (i, 0))
    vec = _resident((1, D_MODEL))
    return pl.pallas_call(
        _out_ffn_kernel,
        grid=(rows // tm,),
        in_specs=[row_block] * 5 + [_resident(wo.shape), _resident(w1.shape), _resident(w2.shape)]
        + [vec] * 4,
        out_specs=row_block,
        out_shape=jax.ShapeDtypeStruct((rows, D_MODEL), F32),
        compiler_params=_cparams(1),
        name="out_ffn",
    )(x2d, o_a, o_b, g_a, g_b, wo, w1, w2, ln1g, ln1b, ln2g, ln2b)


def _gla_gates(lr, wa2_ref, ba_ref):
    z = _dot(lr.astype(BF16), wa2_ref[...]) + ba_ref[...]
    return _log_sigmoid(z) / GLA_TAU


def _gla_out_norm(o, g, r):
    o = o * lax.rsqrt(jnp.mean(o * o, -1, keepdims=True) + LN_EPS) * g
    return o * _silu(r)


def _gla_prompt_kernel(qk_ref, v_ref, r_ref, lr_ref, wa2_ref, ba_ref, g_ref, o_ref, s_ref):
    c = GLA_CHUNK
    sub = GLA_SUBCHUNK
    hk = GLA_HEADS * GLA_DK

    @pl.when(pl.program_id(0) == 0)
    def _():
        s_ref[...] = jnp.zeros(s_ref.shape, F32)

    row = lax.broadcasted_iota(jnp.int32, (c, c), 0)
    col = lax.broadcasted_iota(jnp.int32, (c, c), 1)
    causal = row >= col
    diag_mask = jnp.logical_and(causal, (row // sub) == (col // sub))
    tri = jnp.where(causal, 1.0, 0.0).astype(BF16)
    tri_sub = jnp.where(diag_mask, 1.0, 0.0).astype(BF16)
    off_masks = [jnp.logical_and(col // sub == j, row // sub > j) for j in range(c // sub - 1)]
    nt = (((1,), (1,)), ((), ()))
    for ci in range(GLA_STEP // c):
        rows = slice(ci * c, (ci + 1) * c)
        la = _gla_gates(lr_ref[rows, :], wa2_ref, ba_ref)
        la_hi, la_mid, la_lo = _split3(la)
        b = _dot(tri, la_hi) + (_dot(tri, la_mid) + _dot(tri, la_lo))
        d = _dot(tri_sub, la_hi) + (_dot(tri_sub, la_mid) + _dot(tri_sub, la_lo))
        for h in range(GLA_HEADS):
            kcols = slice(h * GLA_DK, (h + 1) * GLA_DK)
            vcols = slice(h * GLA_DV, (h + 1) * GLA_DV)
            bh = b[:, kcols]
            dh = d[:, kcols]
            qh = qk_ref[rows, kcols] * (GLA_DK ** -0.5)
            kh = qk_ref[rows, hk + h * GLA_DK: hk + (h + 1) * GLA_DK]
            vh = v_ref[rows, vcols].astype(BF16)
            s_old = s_ref[h]
            qt = (qh * jnp.exp(bh)).astype(BF16)
            q_in = (qh * jnp.exp(dh)).astype(BF16)
            k_in = (kh * jnp.exp(-dh)).astype(BF16)
            att = jnp.where(diag_mask, lax.dot_general(q_in, k_in, nt, preferred_element_type=F32), 0.0)
            ends = [bh[(j + 1) * sub - 1:(j + 1) * sub, :] for j in range(c // sub)]
            to_end = jnp.concatenate([dh[(j + 1) * sub - 1:(j + 1) * sub, :] - dh[j * sub:(j + 1) * sub, :]
                                      for j in range(c // sub)], axis=0)
            k_out = (kh * jnp.exp(to_end)).astype(BF16)
            for j, mask in enumerate(off_masks):
                q_out = (qh * jnp.exp(jnp.minimum(bh - ends[j], 0.0))).astype(BF16)
                att = att + jnp.where(mask, lax.dot_general(q_out, k_out, nt, preferred_element_type=F32), 0.0)
            o = _dot(qt, s_old.astype(BF16)) + _dot(att.astype(BF16), vh)
            b_last = ends[-1]
            kd_t = (kh * jnp.exp(b_last - bh)).T.astype(BF16)
            decay = jnp.exp(jnp.broadcast_to(b_last, (SUBLANE, GLA_DK))).T[:, :1]
            s_ref[h] = s_old * decay + _dot(kd_t, vh)
            o_ref[rows, vcols] = _gla_out_norm(o, g_ref[...], r_ref[rows, vcols])


def _gla_prompt(qk, v, r, lr, wa2, ba, g):
    t = qk.shape[0]
    step = lambda width: pl.BlockSpec((GLA_STEP, width), lambda i: (i, 0))
    state_shape = (GLA_HEADS, GLA_DK, GLA_DV)
    return pl.pallas_call(
        _gla_prompt_kernel,
        grid=(t // GLA_STEP,),
        in_specs=[step(D_MODEL), step(D_MODEL), step(D_MODEL), step(LANE),
                  _resident(wa2.shape), _resident(ba.shape), _resident(g.shape)],
        out_specs=[step(D_MODEL), pl.BlockSpec(state_shape, lambda i: (0, 0, 0))],
        out_shape=[jax.ShapeDtypeStruct((t, D_MODEL), F32), jax.ShapeDtypeStruct(state_shape, F32)],
        compiler_params=_cparams(1),
        name="gla_prompt",
    )(qk, v, r, lr, wa2, ba, g)


def _gla_decode_kernel(qk_ref, v_ref, r_ref, lr_ref, wa2_ref, ba_ref, g_ref, s0_ref,
                       o_ref, s_ref, o_scr):
    hk = GLA_HEADS * GLA_DK
    a = jnp.exp(_gla_gates(lr_ref[...], wa2_ref, ba_ref))
    for h in range(GLA_HEADS):
        kcols = slice(h * GLA_DK, (h + 1) * GLA_DK)
        vcols = slice(h * GLA_DV, (h + 1) * GLA_DV)
        a_t = a[:, kcols].T
        q_t = (qk_ref[:, kcols] * (GLA_DK ** -0.5)).T
        k_t = qk_ref[:, hk + h * GLA_DK: hk + (h + 1) * GLA_DK].T
        for n in range(GLA_DECODE_SEQS):
            s_new = s0_ref[n, h] * a_t[:, n:n + 1] + k_t[:, n:n + 1] * v_ref[n:n + 1, vcols]
            s_ref[n, h] = s_new
            o_scr[n:n + 1, vcols] = jnp.sum(q_t[:, n:n + 1] * s_new, axis=0, keepdims=True)
    for h in range(GLA_HEADS):
        vcols = slice(h * GLA_DV, (h + 1) * GLA_DV)
        o_ref[:, vcols] = _gla_out_norm(o_scr[:, vcols], g_ref[...], r_ref[:, vcols])


def _gla_decode(qk, v, r, lr, wa2, ba, g, s0):
    n = qk.shape[0]
    nb = GLA_DECODE_SEQS
    rows = lambda width: pl.BlockSpec((nb, width), lambda i: (i, 0))
    state = pl.BlockSpec((nb, GLA_HEADS, GLA_DK, GLA_DV), lambda i: (i, 0, 0, 0))
    return pl.pallas_call(
        _gla_decode_kernel,
        grid=(n // nb,),
        in_specs=[rows(D_MODEL), rows(D_MODEL), rows(D_MODEL), rows(LANE),
                  _resident(wa2.shape), _resident(ba.shape), _resident(g.shape), state],
        out_specs=[rows(D_MODEL), state],
        out_shape=[jax.ShapeDtypeStruct((n, D_MODEL), F32), jax.ShapeDtypeStruct(s0.shape, F32)],
        scratch_shapes=[pltpu.VMEM((nb, D_MODEL), F32)],
        compiler_params=_cparams(1),
        name="gla_decode",
    )(qk, v, r, lr, wa2, ba, g, s0)


def _moba_prep_kernel(k_ref, v_ref, k16_ref, vt16_ref, mean_ref):
    k = k_ref[...]
    k16_ref[...] = k.astype(BF16)
    mean_ref[...] = jnp.sum(k, axis=0, keepdims=True) / MOBA_BLOCK
    v_t = v_ref[...].T.astype(BF16)
    for h in range(MOBA_HEADS):
        vt16_ref[h, :MOBA_DH, :] = v_t[h * MOBA_DH:(h + 1) * MOBA_DH]
    vt16_ref[:, MOBA_DH:, :] = jnp.ones((MOBA_HEADS, VT_ROWS - MOBA_DH, MOBA_BLOCK), BF16)


def _moba_prep(k, v):
    t = k.shape[0]
    nb = t // MOBA_BLOCK
    blk = pl.BlockSpec((MOBA_BLOCK, D_MODEL), lambda i: (i, 0))
    k16, vt16, means = pl.pallas_call(
        _moba_prep_kernel,
        grid=(nb,),
        in_specs=[blk, blk],
        out_specs=[blk, pl.BlockSpec((MOBA_HEADS, VT_ROWS, MOBA_BLOCK), lambda i: (0, 0, i)),
                   pl.BlockSpec((None, 1, D_MODEL), lambda i: (i, 0, 0))],
        out_shape=[jax.ShapeDtypeStruct((t, D_MODEL), BF16),
                   jax.ShapeDtypeStruct((MOBA_HEADS, VT_ROWS, t), BF16),
                   jax.ShapeDtypeStruct((nb, 1, D_MODEL), F32)],
        compiler_params=_cparams(1),
        name="moba_prep",
    )(k, v)
    return k16, vt16, means.reshape(nb, D_MODEL)


def _select_topk_rows(scores, valid):
    n = scores.shape[0]
    blk = lax.broadcasted_iota(jnp.int32, scores.shape, 0)
    sc = jnp.where(valid, scores, -jnp.inf)
    sel = jnp.zeros(scores.shape, F32)
    picks = []
    for _ in range(MOBA_TOPK):
        mx = jnp.max(sc, axis=0, keepdims=True)
        idx = jnp.min(jnp.where(sc == mx, blk, n), axis=0, keepdims=True)
        hit = blk == idx
        sel = jnp.where(jnp.logical_and(hit, valid), 1.0, sel)
        sc = jnp.where(hit, -jnp.inf, sc)
        picks.append(idx)
    return sel, picks


def _query_block(step, nb):
    half = lax.shift_right_logical(step, 1)
    return jnp.where(jnp.bitwise_and(step, 1) == 0, half, nb - 1 - half)


def _page_copies(pt_ref, cache_ref, buf, sems, layer, step, slot):
    for u in range(PAGES_PER_STEP):
        page = pt_ref[step * PAGES_PER_STEP + u]
        yield pltpu.make_async_copy(cache_ref.at[layer, page], buf.at[slot, u], sems.at[slot])


def _moba_prompt_kernel(pt_ref, cfar_ref, q_ref, k_ref, vt_ref, mean_ref, bnear_ref, bdiag_ref, cache_ref,
                        o_ref, sums_ref, sel_ref, acc_ref, s0_ref, s1_ref, snd_ref, page_buf, page_sems,
                        *, layer, n_sum_steps):
    h = pl.program_id(0)
    nb = sel_ref.shape[0]
    i = _query_block(pl.program_id(1), nb)
    blk_q = MOBA_BLOCK
    step = h * nb + pl.program_id(1)

    @pl.when(step == 0)
    def _():
        for cp in _page_copies(pt_ref, cache_ref, page_buf, page_sems, layer, step, 0):
            cp.start()

    @pl.when(step + 1 < n_sum_steps)
    def _():
        for cp in _page_copies(pt_ref, cache_ref, page_buf, page_sems, layer, step + 1, lax.rem(step + 1, 2)):
            cp.start()

    q_t = q_ref[...].T
    scores = _dot_f32ish(mean_ref[...], q_t)
    blk = lax.broadcasted_iota(jnp.int32, scores.shape, 0)
    sel, _ = _select_topk_rows(scores, blk < i)
    sel_ref[...] = sel
    qs = (q_t * (MOBA_DH ** -0.5 * LOG2E)).astype(BF16)
    acc_ref[...] = jnp.zeros(acc_ref.shape, F32)
    n_far = jnp.maximum(i - 1, 0)
    cfar = cfar_ref[h]

    def block_off(j):
        return pl.multiple_of(j * MOBA_BLOCK, MOBA_BLOCK)

    def block_rows(g):
        return slice(g * MOBA_BLOCK, (g + 1) * MOBA_BLOCK)

    def logits(j):
        return _dot(k_ref[pl.ds(block_off(j), MOBA_BLOCK), :], qs)

    def absorb(m_old, j, s, selected, shift_const):
        cm = jnp.max(s, axis=0, keepdims=True) + shift_const
        m_new = jnp.where(selected, jnp.maximum(m_old, cm), m_old)
        alpha = jnp.exp2(m_old - m_new)
        p = jnp.exp2(s - jnp.where(selected, m_new - shift_const, BIG))
        pv = _dot(vt_ref[:, pl.ds(block_off(j), MOBA_BLOCK)], p.astype(BF16))
        acc_ref[...] = alpha * acc_ref[...] + pv
        return m_new

    def far_trip(trip, m, src_ref, dst_ref):
        j_next = jnp.minimum((trip + 1) * FAR_BLOCKS, nb - FAR_BLOCKS)
        for g in range(FAR_BLOCKS):
            dst_ref[block_rows(g), :] = logits(j_next + g)
            j = trip * FAR_BLOCKS + g
            sel_g = jnp.logical_and(sel_ref[pl.ds(j, 1), :] > 0.5, j < n_far)
            m = absorb(m, j, src_ref[block_rows(g), :], sel_g, cfar)
        return m

    def far_body(trip, m):
        return lax.cond(lax.rem(trip, 2) == 0,
                        lambda mm: far_trip(trip, mm, s0_ref, s1_ref),
                        lambda mm: far_trip(trip, mm, s1_ref, s0_ref), m)

    for g in range(FAR_BLOCKS):
        s0_ref[block_rows(g), :] = logits(g)
    snd_ref[block_rows(0), :] = logits(n_far) + bnear_ref[...]
    snd_ref[block_rows(1), :] = logits(i) + bdiag_ref[...]
    m = jnp.full((1, blk_q), NEG, F32)
    m = lax.fori_loop(0, lax.div(n_far + (FAR_BLOCKS - 1), FAR_BLOCKS), far_body, m)

    def reduce_pages():
        slot = lax.rem(step, 2)
        for cp in _page_copies(pt_ref, cache_ref, page_buf, page_sems, layer, step, slot):
            cp.wait()
        pages_per_block = MOBA_BLOCK // PAGE_SIZE
        for b in range(PAGES_PER_STEP // pages_per_block):
            total = jnp.sum(page_buf[slot, b * pages_per_block], axis=0)
            for u in range(1, pages_per_block):
                total = total + jnp.sum(page_buf[slot, b * pages_per_block + u], axis=0)
            sums_ref[b] = total

    every_step = n_sum_steps == MOBA_HEADS * nb
    if every_step:
        reduce_pages()
    near_sel = jnp.logical_and(sel_ref[pl.ds(n_far, 1), :] > 0.5, i >= 1)
    m = absorb(m, n_far, snd_ref[block_rows(0), :], near_sel, 0.0)
    absorb(m, i, snd_ref[block_rows(1), :], True, 0.0)
    o_ref[...] = (acc_ref[:MOBA_DH, :] / acc_ref[MOBA_DH:MOBA_DH + 1, :]).T
    if not every_step:
        pl.when(step < n_sum_steps)(reduce_pages)


def _moba_prompt(q, k16, vt16, means, bnear_t, bdiag_t, cfar, cache_k, layer, page_table):
    t = q.shape[0]
    nb = t // MOBA_BLOCK
    n_seq, n_pages = page_table.shape
    n_sum_steps = n_seq * n_pages // PAGES_PER_STEP
    blocks_per_step = PAGES_PER_STEP // (MOBA_BLOCK // PAGE_SIZE)
    assert nb % FAR_BLOCKS == 0 and n_sum_steps <= MOBA_HEADS * nb
    bias_spec = pl.BlockSpec((None, MOBA_BLOCK, MOBA_BLOCK), lambda h, s, pt: (h, 0, 0))
    tile_spec = pl.BlockSpec((MOBA_BLOCK, MOBA_DH), lambda h, s, pt: (_query_block(s, nb), h))
    trip_logits = pltpu.VMEM((FAR_BLOCKS * MOBA_BLOCK, MOBA_BLOCK), F32)
    o_b, sums = pl.pallas_call(
        functools.partial(_moba_prompt_kernel, layer=layer, n_sum_steps=n_sum_steps),
        grid_spec=pltpu.PrefetchScalarGridSpec(
            num_scalar_prefetch=1,
            grid=(MOBA_HEADS, nb),
            in_specs=[pl.BlockSpec(memory_space=pltpu.SMEM),
                      tile_spec,
                      pl.BlockSpec((t, MOBA_DH), lambda h, s, pt: (0, h)),
                      pl.BlockSpec((None, VT_ROWS, t), lambda h, s, pt: (h, 0, 0)),
                      pl.BlockSpec((nb, MOBA_DH), lambda h, s, pt: (0, h)),
                      bias_spec, bias_spec,
                      pl.BlockSpec(memory_space=pl.ANY)],
            out_specs=[tile_spec,
                       pl.BlockSpec((None, blocks_per_step, MOBA_HEADS, MOBA_DH),
                                    lambda h, s, pt: (jnp.minimum(h * nb + s, n_sum_steps - 1), 0, 0, 0))],
            scratch_shapes=[pltpu.VMEM((nb, MOBA_BLOCK), F32), pltpu.VMEM((VT_ROWS, MOBA_BLOCK), F32),
                            trip_logits, trip_logits, pltpu.VMEM((2 * MOBA_BLOCK, MOBA_BLOCK), F32),
                            pltpu.VMEM((2, PAGES_PER_STEP, PAGE_SIZE, MOBA_HEADS, MOBA_DH), F32),
                            pltpu.SemaphoreType.DMA((2,))],
        ),
        out_shape=[jax.ShapeDtypeStruct((t, D_MODEL), F32),
                   jax.ShapeDtypeStruct((n_sum_steps, blocks_per_step, MOBA_HEADS, MOBA_DH), F32)],
        compiler_params=_cparams(2),
        name="moba_prompt",
    )(page_table.reshape(-1), cfar, q, k16, vt16, means, bnear_t, bdiag_t, cache_k)
    return o_b, sums.reshape(n_seq, n_pages * PAGE_SIZE // MOBA_BLOCK, MOBA_HEADS, MOBA_DH)


def _select_kernel(q_ref, sums_ref, out_ref):
    nbp = sums_ref.shape[1]
    ones = jnp.ones((MOBA_DH, LANE), BF16)
    for n in range(SELECT_SEQS):
        prod = (sums_ref[n] / MOBA_BLOCK) * q_ref[n]
        p_hi, p_mid, p_lo = _split3(prod.reshape(nbp * MOBA_HEADS, MOBA_DH))
        scores = _dot(p_hi, ones) + (_dot(p_mid, ones) + _dot(p_lo, ones))
        _, picks = _select_topk_rows(scores.reshape(nbp, MOBA_HEADS, LANE), True)
        for r, idx in enumerate(picks):
            out_ref[n, r] = idx[0]


def _select_blocks(q_heads, sums):
    n, nbp = sums.shape[:2]
    sb = SELECT_SEQS
    return pl.pallas_call(
        _select_kernel,
        grid=(n // sb,),
        in_specs=[pl.BlockSpec((sb, MOBA_HEADS, MOBA_DH), lambda s: (s, 0, 0)),
                  pl.BlockSpec((sb, nbp, MOBA_HEADS, MOBA_DH), lambda s: (s, 0, 0, 0))],
        out_specs=pl.BlockSpec((sb, MOBA_TOPK, MOBA_HEADS, LANE), lambda s: (s, 0, 0, 0)),
        out_shape=jax.ShapeDtypeStruct((n, MOBA_TOPK, MOBA_HEADS, LANE), jnp.int32),
        compiler_params=_cparams(1),
        name="select_blocks",
    )(q_heads, sums)


def _moba_decode_kernel(pages_ref, selblk_ref, bias_s_ref, q_ref, kn_ref, vn_ref, blast_ref,
                        ck_ref, cv_ref, o_ref, kbuf, vbuf, sems, *, layer, newest_block):
    pages_per_block = MOBA_BLOCK // PAGE_SIZE
    n_sel = MOBA_TOPK * pages_per_block
    seq = pl.program_id(0)
    slot = seq % 2

    def tile_copies(which_seq, which_slot):
        for h in range(MOBA_HEADS):
            for r in range(n_sel):
                page = pages_ref[(which_seq * MOBA_HEADS + h) * n_sel + r]
                dst_rows = pl.ds(r * PAGE_SIZE, PAGE_SIZE)
                yield pltpu.make_async_copy(ck_ref.at[layer, page, :, h, :], kbuf.at[which_slot, h, dst_rows],
                                            sems.at[which_slot])
                yield pltpu.make_async_copy(cv_ref.at[layer, page, :, h, :], vbuf.at[which_slot, h, dst_rows],
                                            sems.at[which_slot])

    @pl.when(seq == 0)
    def _():
        for c, cp in enumerate(tile_copies(seq, slot)):
            cp.start(priority=c % 2)

    @pl.when(seq + 1 < pl.num_programs(0))
    def _():
        for c, cp in enumerate(tile_copies(seq + 1, 1 - slot)):
            cp.start(priority=c % 2)

    for cp in tile_copies(seq, slot):
        cp.wait()

    logits, owns = [], []
    for h in range(MOBA_HEADS):
        q8 = jnp.broadcast_to(q_ref[h:h + 1, :] * (MOBA_DH ** -0.5), (SUBLANE, MOBA_DH))
        s = lax.dot_general(q8.astype(BF16), kbuf[slot, h].astype(BF16), (((1,), (1,)), ((), ())),
                            preferred_element_type=F32)
        bias = []
        for r in range(MOBA_TOPK):
            is_newest = selblk_ref[(seq * MOBA_HEADS + h) * MOBA_TOPK + r] == newest_block
            bias.append(jnp.where(is_newest, blast_ref[h], bias_s_ref[0, h]))
        logits.append(s + jnp.concatenate(bias, axis=-1))
        owns.append(jnp.sum(q8 * kn_ref[h:h + 1, :], axis=-1, keepdims=True) + bias_s_ref[1, h])
    probs = []
    for s, s_own in zip(logits, owns):
        m = jnp.maximum(s_own, jnp.max(s, axis=-1, keepdims=True))
        p = jnp.exp(s - m)
        p_own = jnp.exp(s_own - m)
        probs.append((p.astype(BF16), p_own, p_own + jnp.sum(p, axis=-1, keepdims=True)))
    for h, (p, p_own, l) in enumerate(probs):
        out = p_own * vn_ref[h:h + 1, :] + _dot(p, vbuf[slot, h].astype(BF16))
        o_ref[h:h + 1, :] = (out / l)[:1]


def _moba_decode(pages, selblk, bias_s, q_heads, kn_heads, vn_heads, blast, cache_k, cache_v, layer,
                 newest_block):
    n = q_heads.shape[0]
    n_sel = MOBA_TOPK * (MOBA_BLOCK // PAGE_SIZE)
    tok = pl.BlockSpec((None, MOBA_HEADS, MOBA_DH), lambda s, pg, sb: (s, 0, 0))
    tiles = pltpu.VMEM((2, MOBA_HEADS, n_sel * PAGE_SIZE, MOBA_DH), F32)
    return pl.pallas_call(
        functools.partial(_moba_decode_kernel, layer=layer, newest_block=newest_block),
        grid_spec=pltpu.PrefetchScalarGridSpec(
            num_scalar_prefetch=2,
            grid=(n,),
            in_specs=[pl.BlockSpec(memory_space=pltpu.SMEM), tok, tok, tok, _resident(blast.shape),
                      pl.BlockSpec(memory_space=pl.ANY), pl.BlockSpec(memory_space=pl.ANY)],
            out_specs=tok,
            scratch_shapes=[tiles, tiles, pltpu.SemaphoreType.DMA((2,))],
        ),
        out_shape=jax.ShapeDtypeStruct((n, MOBA_HEADS, MOBA_DH), F32),
        compiler_params=_cparams(1),
        name="moba_decode",
    )(pages, selblk, bias_s, q_heads, kn_heads, vn_heads, blast, cache_k, cache_v)


def _rel_bucket_table(n):
    d = np.arange(n)
    max_exact = REL_BUCKETS // 2
    big = max_exact + (np.log(np.maximum(d, max_exact).astype(np.float32) / max_exact)
                       / math.log(REL_MAX_DIST / max_exact) * (REL_BUCKETS - max_exact)).astype(np.int32)
    return np.where(d < max_exact, d, np.minimum(big, REL_BUCKETS - 1)).astype(np.int32)


def _bias_tables(rel_bias):
    bucket = _rel_bucket_table(2 * MOBA_BLOCK)
    assert bucket[MOBA_BLOCK + 1:].min() == REL_BUCKETS - 1
    one_hot = (bucket[:, None] == np.arange(REL_BUCKETS)[None, :]).astype(np.float32)
    by_dist = jnp.dot(one_hot, rel_bias, precision=lax.Precision.HIGHEST).T
    n = MOBA_BLOCK

    def toeplitz_t(g):
        g = jnp.pad(g, ((0, 0), (0, 1)))
        shifted = jnp.tile(g, (1, n))[:, :n * 2 * n].reshape(MOBA_HEADS, n, 2 * n)
        return shifted[:, :, n:]

    near_t = toeplitz_t(by_dist)
    diag_t = toeplitz_t(jnp.concatenate([jnp.full((MOBA_HEADS, n), NEG, F32), by_dist[:, :n]], axis=1))
    far = rel_bias[REL_BUCKETS - 1, :]
    last_blk = by_dist[:, n:0:-1]
    return near_t, diag_t, far, last_blk.reshape(MOBA_HEADS, 1, MOBA_BLOCK), rel_bias[0, :]


def kernel(x_prompt, x_sample, cache_k, cache_v, state_gla, page_table, w_in, w_alpha2, b_alpha,
           g_gla_norm, rel_bias, w_out, ln1_g, ln1_b, w_ffn_in, w_ffn_out, ln2_g, ln2_b):
    layer = 0
    n_batch, t, _ = x_prompt.shape
    n_seq, t_new, _ = x_sample.shape
    n_pages = page_table.shape[1]
    past = n_pages * PAGE_SIZE
    assert n_batch == 1 and t_new == 1 and t % MOBA_BLOCK == 0 and t % GLA_STEP == 0
    assert past % MOBA_BLOCK == 0 and past // MOBA_BLOCK >= MOBA_TOPK
    assert n_pages % PAGES_PER_STEP == 0 and n_seq % GLA_DECODE_SEQS == 0 and n_seq % SELECT_SEQS == 0
    nbp = past // MOBA_BLOCK

    hk = GLA_HEADS * GLA_DK
    lr0 = 2 * hk + 2 * D_MODEL
    w = w_in[layer]
    w_main = jnp.concatenate([w[:, :lr0], w[:, lr0 + GLA_LOWRANK:]], axis=1).astype(BF16)
    w_lr = jnp.pad(w[:, lr0:lr0 + GLA_LOWRANK], ((0, 0), (0, LANE - GLA_LOWRANK))).astype(BF16)
    wa2 = jnp.pad(w_alpha2[layer], ((0, LANE - GLA_LOWRANK), (0, 0))).astype(BF16)
    ba = b_alpha[layer].reshape(1, hk)
    g = g_gla_norm[layer].reshape(1, GLA_DV)
    wo = w_out[layer].astype(BF16)
    w1 = w_ffn_in[layer].astype(BF16)
    w2 = w_ffn_out[layer].astype(BF16)
    ln = [a[layer].reshape(1, D_MODEL) for a in (ln1_g, ln1_b, ln2_g, ln2_b)]
    near_t, diag_t, bias_far, bias_last, bias_own = _bias_tables(rel_bias)

    xp = x_prompt.reshape(t, D_MODEL)
    qk_a, v_a, r_a, q_b, k_b, v_b, g_a, g_b, lr = _in_proj(xp, w_main, w_lr)
    o_a, s_prompt = _gla_prompt(qk_a, v_a, r_a, lr, wa2, ba, g)
    k16, vt16, means = _moba_prep(k_b, v_b)
    o_b, sums = _moba_prompt(q_b, k16, vt16, means, near_t * LOG2E, diag_t * LOG2E, bias_far * LOG2E,
                             cache_k, layer, page_table)
    y_prompt = _out_ffn(xp, o_a, o_b, g_a, g_b, wo, w1, w2, *ln)

    xs = x_sample.reshape(n_seq, D_MODEL)
    qk_s, v_s, r_s, qb_s, kb_s, vb_s, ga_s, gb_s, lr_s = _in_proj(xs, w_main, w_lr)
    oa_s, s_sample = _gla_decode(qk_s, v_s, r_s, lr_s, wa2, ba, g, state_gla[layer])
    heads = lambda a: a.reshape(n_seq, MOBA_HEADS, MOBA_DH)
    picks = _select_blocks(heads(qb_s), sums)[..., 0]
    picks = picks.transpose(0, 2, 1)
    ppb = MOBA_BLOCK // PAGE_SIZE
    logical = (picks[..., None] * ppb + jnp.arange(ppb)).reshape(n_seq, -1)
    pages = jnp.take_along_axis(page_table, logical, axis=1).reshape(-1)
    bias_s = jnp.stack([bias_far, bias_own])
    ob_s = _moba_decode(pages, picks.reshape(-1), bias_s, heads(qb_s), heads(kb_s), heads(vb_s), bias_last,
                        cache_k, cache_v, layer, nbp - 1)
    y_sample = _out_ffn(xs, oa_s, ob_s.reshape(n_seq, D_MODEL), ga_s, gb_s, wo, w1, w2, *ln)

    kv_shape = lambda n, tt: (DEPTH, n, tt, MOBA_HEADS, MOBA_DH)
    return (y_prompt.reshape(1, t, D_MODEL), y_sample.reshape(n_seq, 1, D_MODEL),
            k_b.reshape(kv_shape(1, t)), v_b.reshape(kv_shape(1, t)),
            s_prompt.reshape(DEPTH, 1, GLA_HEADS, GLA_DK, GLA_DV),
            kb_s.reshape(kv_shape(n_seq, 1)), vb_s.reshape(kv_shape(n_seq, 1)),
            s_sample.reshape(DEPTH, n_seq, GLA_HEADS, GLA_DK, GLA_DV))
```

```python
import functools
import math

import jax
import jax.numpy as jnp
import numpy as np
from jax import lax
from jax.experimental import pallas as pl
from jax.experimental.pallas import tpu as pltpu

D_MODEL = 1024
DEPTH = 1
GLA_HEADS = 4
GLA_DK = 128
GLA_DV = 256
GLA_LOWRANK = 16
GLA_TAU = 16.0
GLA_CHUNK = 64
MOBA_HEADS = 8
MOBA_DH = 128
MOBA_BLOCK = 256
MOBA_TOPK = 3
PAGE_SIZE = 128
REL_BUCKETS = 32
REL_MAX_DIST = 128
DN_ALPHA = (2 * DEPTH) ** 0.25
LN_EPS = 1e-5

LANE = 128
SUBLANE = 8
VMEM_LIMIT_BYTES = 56 * 1024 * 1024

NEG = -1e30
BIG = 1e30

ROW_TILE = 256
GLA_STEP = 256
GLA_SUBCHUNK = 16
GLA_DECODE_SEQS = 8
SELECT_SEQS = 8
PAGES_PER_STEP = 16
PAGE_LOOKAHEAD = 2
PAGE_SLOTS = PAGE_LOOKAHEAD + 1
FAR_BLOCKS = 8
VT_ROWS = 128 + 16

LOG2E = math.log2(math.e)

F32 = jnp.float32
BF16 = jnp.bfloat16


def _cparams(n_axes):
    return pltpu.CompilerParams(dimension_semantics=("arbitrary",) * n_axes,
                                vmem_limit_bytes=VMEM_LIMIT_BYTES)


def _resident(shape):
    return pl.BlockSpec(shape, lambda *_: (0,) * len(shape), pipeline_mode=pl.Buffered(1))


def _split3(a):
    hi = a.astype(BF16)
    r = a - hi.astype(F32)
    mid = r.astype(BF16)
    lo = (r - mid.astype(F32)).astype(BF16)
    return hi, mid, lo


def _dot(a, b):
    return jnp.dot(a, b, preferred_element_type=F32)


def _dot_f32ish(a, b):
    ah, al, _ = _split3(a)
    bh, bl, _ = _split3(b)
    return _dot(ah, bh) + (_dot(ah, bl) + _dot(al, bh))


def _log_sigmoid(z):
    return jnp.minimum(z, 0.0) - jnp.log1p(jnp.exp(-jnp.abs(z)))


def _silu(x):
    return x * jax.nn.sigmoid(x)


def _layer_norm(x, g, b):
    mu = jnp.mean(x, -1, keepdims=True)
    xc = x - mu
    var = jnp.mean(xc * xc, -1, keepdims=True)
    return xc * lax.rsqrt(var + LN_EPS) * g + b


def _in_proj_kernel(x_ref, w_ref, wlr_ref, *out_refs):
    xb = x_ref[...].astype(BF16)
    for g, o_ref in enumerate(out_refs[:-1]):
        o_ref[...] = _dot(xb, w_ref[:, g * D_MODEL:(g + 1) * D_MODEL])
    out_refs[-1][...] = _dot(xb, wlr_ref[...])


def _in_proj(x2d, w_main, w_lr):
    rows = x2d.shape[0]
    tm = min(ROW_TILE, rows)
    groups = w_main.shape[1] // D_MODEL
    row_block = lambda width: pl.BlockSpec((tm, width), lambda i: (i, 0))
    return pl.pallas_call(
        _in_proj_kernel,
        grid=(rows // tm,),
        in_specs=[row_block(D_MODEL), _resident(w_main.shape), _resident(w_lr.shape)],
        out_specs=[row_block(D_MODEL)] * groups + [row_block(LANE)],
        out_shape=[jax.ShapeDtypeStruct((rows, D_MODEL), F32)] * groups
        + [jax.ShapeDtypeStruct((rows, LANE), F32)],
        compiler_params=_cparams(1),
        name="in_proj",
    )(x2d, w_main, w_lr)


def _out_ffn_kernel(x_ref, oa_ref, ob_ref, ga_ref, gb_ref, wo_ref, w1_ref, w2_ref,
                    ln1g_ref, ln1b_ref, ln2g_ref, ln2b_ref, y_ref):
    d_ff = w2_ref.shape[0]
    merged = jax.nn.sigmoid(ga_ref[...]) * oa_ref[...] + jax.nn.sigmoid(gb_ref[...]) * ob_ref[...]
    mix = _dot(merged.astype(BF16), wo_ref[...])
    x1 = _layer_norm(DN_ALPHA * x_ref[...] + mix, ln1g_ref[...], ln1b_ref[...])
    hidden = _dot(x1.astype(BF16), w1_ref[...])
    act = _silu(hidden[:, :d_ff]) * hidden[:, d_ff:]
    ffn = _dot(act.astype(BF16), w2_ref[...])
    y_ref[...] = _layer_norm(DN_ALPHA * x1 + ffn, ln2g_ref[...], ln2b_ref[...])


def _out_ffn(x2d, o_a, o_b, g_a, g_b, wo, w1, w2, ln1g, ln1b, ln2g, ln2b):
    rows = x2d.shape[0]
    tm = min(ROW_TILE, rows)
    row_block = pl.BlockSpec((tm, D_MODEL), lambda i: (i, 0))
    vec = _resident((1, D_MODEL))
    return pl.pallas_call(
        _out_ffn_kernel,
        grid=(rows // tm,),
        in_specs=[row_block] * 5 + [_resident(wo.shape), _resident(w1.shape), _resident(w2.shape)]
        + [vec] * 4,
        out_specs=row_block,
        out_shape=jax.ShapeDtypeStruct((rows, D_MODEL), F32),
        compiler_params=_cparams(1),
        name="out_ffn",
    )(x2d, o_a, o_b, g_a, g_b, wo, w1, w2, ln1g, ln1b, ln2g, ln2b)


def _gla_gates(lr, wa2_ref, ba_ref):
    z = _dot(lr.astype(BF16), wa2_ref[...]) + ba_ref[...]
    return _log_sigmoid(z) / GLA_TAU


def _gla_out_norm(o, g, r):
    o = o * lax.rsqrt(jnp.mean(o * o, -1, keepdims=True) + LN_EPS) * g
    return o * _silu(r)


def _gla_prompt_kernel(qk_ref, v_ref, r_ref, lr_ref, wa2_ref, ba_ref, g_ref, o_ref, s_ref):
    c = GLA_CHUNK
    sub = GLA_SUBCHUNK
    hk = GLA_HEADS * GLA_DK

    @pl.when(pl.program_id(0) == 0)
    def _():
        s_ref[...] = jnp.zeros(s_ref.shape, F32)

    row = lax.broadcasted_iota(jnp.int32, (c, c), 0)
    col = lax.broadcasted_iota(jnp.int32, (c, c), 1)
    causal = row >= col
    diag_mask = jnp.logical_and(causal, (row // sub) == (col // sub))
    off_masks = [jnp.logical_and(col // sub == j, row // sub > j) for j in range(c // sub - 1)]
    nt = (((1,), (1,)), ((), ()))
    srow = lax.broadcasted_iota(jnp.int32, (GLA_STEP, GLA_STEP), 0)
    scol = lax.broadcasted_iota(jnp.int32, (GLA_STEP, GLA_STEP), 1)
    tri = jnp.where(jnp.logical_and(srow >= scol, srow // c == scol // c), 1.0, 0.0).astype(BF16)
    tri_sub = jnp.where(jnp.logical_and(srow >= scol, srow // sub == scol // sub), 1.0, 0.0).astype(BF16)
    la = _gla_gates(lr_ref[...], wa2_ref, ba_ref)
    la_hi, la_mid, la_lo = _split3(la)
    b_all = _dot(tri, la_hi) + (_dot(tri, la_mid) + _dot(tri, la_lo))
    d_all = _dot(tri_sub, la_hi) + (_dot(tri_sub, la_mid) + _dot(tri_sub, la_lo))
    for ci in range(GLA_STEP // c):
        rows = slice(ci * c, (ci + 1) * c)
        b = b_all[rows]
        d = d_all[rows]
        for h in range(GLA_HEADS):
            kcols = slice(h * GLA_DK, (h + 1) * GLA_DK)
            vcols = slice(h * GLA_DV, (h + 1) * GLA_DV)
            bh = b[:, kcols]
            dh = d[:, kcols]
            qh = qk_ref[rows, kcols] * (GLA_DK ** -0.5)
            kh = qk_ref[rows, hk + h * GLA_DK: hk + (h + 1) * GLA_DK]
            vh = v_ref[rows, vcols].astype(BF16)
            s_old = s_ref[h]
            qt = (qh * jnp.exp(bh)).astype(BF16)
            q_in = (qh * jnp.exp(dh)).astype(BF16)
            k_in = (kh * jnp.exp(-dh)).astype(BF16)
            att = jnp.where(diag_mask, lax.dot_general(q_in, k_in, nt, preferred_element_type=F32), 0.0)
            ends = [bh[(j + 1) * sub - 1:(j + 1) * sub, :] for j in range(c // sub)]
            to_end = jnp.concatenate([dh[(j + 1) * sub - 1:(j + 1) * sub, :] - dh[j * sub:(j + 1) * sub, :]
                                      for j in range(c // sub)], axis=0)
            k_out = (kh * jnp.exp(to_end)).astype(BF16)
            for j, mask in enumerate(off_masks):
                q_out = (qh * jnp.exp(jnp.minimum(bh - ends[j], 0.0))).astype(BF16)
                att = att + jnp.where(mask, lax.dot_general(q_out, k_out, nt, preferred_element_type=F32), 0.0)
            o = _dot(qt, s_old.astype(BF16)) + _dot(att.astype(BF16), vh)
            b_last = ends[-1]
            kd_t = (kh * jnp.exp(b_last - bh)).T.astype(BF16)
            decay = jnp.exp(jnp.broadcast_to(b_last, (SUBLANE, GLA_DK))).T[:, :1]
            s_ref[h] = s_old * decay + _dot(kd_t, vh)
            o_ref[rows, vcols] = _gla_out_norm(o, g_ref[...], r_ref[rows, vcols])


def _gla_prompt(qk, v, r, lr, wa2, ba, g):
    t = qk.shape[0]
    step = lambda width: pl.BlockSpec((GLA_STEP, width), lambda i: (i, 0))
    state_shape = (GLA_HEADS, GLA_DK, GLA_DV)
    return pl.pallas_call(
        _gla_prompt_kernel,
        grid=(t // GLA_STEP,),
        in_specs=[step(D_MODEL), step(D_MODEL), step(D_MODEL), step(LANE),
                  _resident(wa2.shape), _resident(ba.shape), _resident(g.shape)],
        out_specs=[step(D_MODEL), pl.BlockSpec(state_shape, lambda i: (0, 0, 0))],
        out_shape=[jax.ShapeDtypeStruct((t, D_MODEL), F32), jax.ShapeDtypeStruct(state_shape, F32)],
        compiler_params=_cparams(1),
        name="gla_prompt",
    )(qk, v, r, lr, wa2, ba, g)


def _gla_decode_kernel(qk_ref, v_ref, r_ref, lr_ref, wa2_ref, ba_ref, g_ref, s0_ref,
                       o_ref, s_ref, o_scr):
    hk = GLA_HEADS * GLA_DK
    a = jnp.exp(_gla_gates(lr_ref[...], wa2_ref, ba_ref))
    for h in range(GLA_HEADS):
        kcols = slice(h * GLA_DK, (h + 1) * GLA_DK)
        vcols = slice(h * GLA_DV, (h + 1) * GLA_DV)
        a_t = a[:, kcols].T
        q_t = (qk_ref[:, kcols] * (GLA_DK ** -0.5)).T
        k_t = qk_ref[:, hk + h * GLA_DK: hk + (h + 1) * GLA_DK].T
        for n in range(GLA_DECODE_SEQS):
            s_new = s0_ref[n, h] * a_t[:, n:n + 1] + k_t[:, n:n + 1] * v_ref[n:n + 1, vcols]
            s_ref[n, h] = s_new
            o_scr[n:n + 1, vcols] = jnp.sum(q_t[:, n:n + 1] * s_new, axis=0, keepdims=True)
    for h in range(GLA_HEADS):
        vcols = slice(h * GLA_DV, (h + 1) * GLA_DV)
        o_ref[:, vcols] = _gla_out_norm(o_scr[:, vcols], g_ref[...], r_ref[:, vcols])


def _gla_decode(qk, v, r, lr, wa2, ba, g, s0):
    n = qk.shape[0]
    nb = GLA_DECODE_SEQS
    rows = lambda width: pl.BlockSpec((nb, width), lambda i: (i, 0))
    state = pl.BlockSpec((nb, GLA_HEADS, GLA_DK, GLA_DV), lambda i: (i, 0, 0, 0))
    return pl.pallas_call(
        _gla_decode_kernel,
        grid=(n // nb,),
        in_specs=[rows(D_MODEL), rows(D_MODEL), rows(D_MODEL), rows(LANE),
                  _resident(wa2.shape), _resident(ba.shape), _resident(g.shape), state],
        out_specs=[rows(D_MODEL), state],
        out_shape=[jax.ShapeDtypeStruct((n, D_MODEL), F32), jax.ShapeDtypeStruct(s0.shape, F32)],
        scratch_shapes=[pltpu.VMEM((nb, D_MODEL), F32)],
        compiler_params=_cparams(1),
        name="gla_decode",
    )(qk, v, r, lr, wa2, ba, g, s0)


def _moba_prep_kernel(k_ref, v_ref, k16_ref, vt16_ref, mean_ref):
    k = k_ref[...]
    k16_ref[...] = k.astype(BF16)
    mean_ref[...] = jnp.sum(k, axis=0, keepdims=True) / MOBA_BLOCK
    v_t = v_ref[...].T.astype(BF16)
    for h in range(MOBA_HEADS):
        vt16_ref[h, :MOBA_DH, :] = v_t[h * MOBA_DH:(h + 1) * MOBA_DH]
    vt16_ref[:, MOBA_DH:, :] = jnp.ones((MOBA_HEADS, VT_ROWS - MOBA_DH, MOBA_BLOCK), BF16)


def _moba_prep(k, v):
    t = k.shape[0]
    nb = t // MOBA_BLOCK
    blk = pl.BlockSpec((MOBA_BLOCK, D_MODEL), lambda i: (i, 0))
    k16, vt16, means = pl.pallas_call(
        _moba_prep_kernel,
        grid=(nb,),
        in_specs=[blk, blk],
        out_specs=[blk, pl.BlockSpec((MOBA_HEADS, VT_ROWS, MOBA_BLOCK), lambda i: (0, 0, i)),
                   pl.BlockSpec((None, 1, D_MODEL), lambda i: (i, 0, 0))],
        out_shape=[jax.ShapeDtypeStruct((t, D_MODEL), BF16),
                   jax.ShapeDtypeStruct((MOBA_HEADS, VT_ROWS, t), BF16),
                   jax.ShapeDtypeStruct((nb, 1, D_MODEL), F32)],
        compiler_params=_cparams(1),
        name="moba_prep",
    )(k, v)
    return k16, vt16, means.reshape(nb, D_MODEL)


def _select_topk_rows(scores, valid):
    n = scores.shape[0]
    blk = lax.broadcasted_iota(jnp.int32, scores.shape, 0)
    sc = jnp.where(valid, scores, -jnp.inf)
    sel = jnp.zeros(scores.shape, F32)
    picks = []
    for _ in range(MOBA_TOPK):
        mx = jnp.max(sc, axis=0, keepdims=True)
        idx = jnp.min(jnp.where(sc == mx, blk, n), axis=0, keepdims=True)
        hit = blk == idx
        sel = jnp.where(jnp.logical_and(hit, valid), 1.0, sel)
        sc = jnp.where(hit, -jnp.inf, sc)
        picks.append(idx)
    return sel, picks


def _query_block(step, nb):
    half = lax.shift_right_logical(step, 1)
    return jnp.where(jnp.bitwise_and(step, 1) == 0, half, nb - 1 - half)


def _page_copies(pt_ref, cache_ref, buf, sems, layer, step, slot):
    for u in range(PAGES_PER_STEP):
        page = pt_ref[step * PAGES_PER_STEP + u]
        yield pltpu.make_async_copy(cache_ref.at[layer, page], buf.at[slot, u], sems.at[slot])


def _moba_prompt_kernel(pt_ref, cfar_ref, q_ref, k_ref, vt_ref, mean_ref, bnear_ref, bdiag_ref, cache_ref,
                        o_ref, sums_ref, sel_ref, acc_ref, s0_ref, s1_ref, snd_ref, page_buf, page_sems,
                        *, layer, n_sum_steps):
    h = pl.program_id(0)
    nb = sel_ref.shape[0]
    i = _query_block(pl.program_id(1), nb)
    blk_q = MOBA_BLOCK
    step = h * nb + pl.program_id(1)

    @pl.when(step == 0)
    def _():
        for ahead in range(PAGE_LOOKAHEAD):
            if ahead < n_sum_steps:
                for cp in _page_copies(pt_ref, cache_ref, page_buf, page_sems, layer, ahead, ahead):
                    cp.start()

    @pl.when(step + PAGE_LOOKAHEAD < n_sum_steps)
    def _():
        nxt = step + PAGE_LOOKAHEAD
        for cp in _page_copies(pt_ref, cache_ref, page_buf, page_sems, layer, nxt, lax.rem(nxt, PAGE_SLOTS)):
            cp.start()

    q_t = q_ref[...].T
    scores = _dot_f32ish(mean_ref[...], q_t)
    blk = lax.broadcasted_iota(jnp.int32, scores.shape, 0)
    sel, _ = _select_topk_rows(scores, blk < i)
    sel_ref[...] = sel
    qs = (q_t * (MOBA_DH ** -0.5 * LOG2E)).astype(BF16)
    acc_ref[...] = jnp.zeros(acc_ref.shape, F32)
    n_far = jnp.maximum(i - 1, 0)
    cfar = cfar_ref[h]

    def block_off(j):
        return pl.multiple_of(j * MOBA_BLOCK, MOBA_BLOCK)

    def block_rows(g):
        return slice(g * MOBA_BLOCK, (g + 1) * MOBA_BLOCK)

    def logits(j):
        return _dot(k_ref[pl.ds(block_off(j), MOBA_BLOCK), :], qs)

    def absorb(m_old, j, s, selected, shift_const):
        cm = jnp.max(s, axis=0, keepdims=True) + shift_const
        m_new = jnp.where(selected, jnp.maximum(m_old, cm), m_old)
        alpha = jnp.exp2(m_old - m_new)
        p = jnp.exp2(s - jnp.where(selected, m_new - shift_const, BIG))
        pv = _dot(vt_ref[:, pl.ds(block_off(j), MOBA_BLOCK)], p.astype(BF16))
        acc_ref[...] = alpha * acc_ref[...] + pv
        return m_new

    def far_trip(trip, m, src_ref, dst_ref):
        j_next = jnp.minimum((trip + 1) * FAR_BLOCKS, nb - FAR_BLOCKS)
        for g in range(FAR_BLOCKS):
            dst_ref[block_rows(g), :] = logits(j_next + g)
            j = trip * FAR_BLOCKS + g
            sel_g = jnp.logical_and(sel_ref[pl.ds(j, 1), :] > 0.5, j < n_far)
            m = absorb(m, j, src_ref[block_rows(g), :], sel_g, cfar)
        return m

    def far_body(trip, m):
        return lax.cond(lax.rem(trip, 2) == 0,
                        lambda mm: far_trip(trip, mm, s0_ref, s1_ref),
                        lambda mm: far_trip(trip, mm, s1_ref, s0_ref), m)

    for g in range(FAR_BLOCKS):
        s0_ref[block_rows(g), :] = logits(g)
    snd_ref[block_rows(0), :] = logits(n_far) + bnear_ref[...]
    snd_ref[block_rows(1), :] = logits(i) + bdiag_ref[...]
    m = jnp.full((1, blk_q), NEG, F32)
    m = lax.fori_loop(0, lax.div(n_far + (FAR_BLOCKS - 1), FAR_BLOCKS), far_body, m)

    def reduce_pages():
        slot = lax.rem(step, PAGE_SLOTS)
        for cp in _page_copies(pt_ref, cache_ref, page_buf, page_sems, layer, step, slot):
            cp.wait()
        pages_per_block = MOBA_BLOCK // PAGE_SIZE
        for b in range(PAGES_PER_STEP // pages_per_block):
            total = jnp.sum(page_buf[slot, b * pages_per_block], axis=0)
            for u in range(1, pages_per_block):
                total = total + jnp.sum(page_buf[slot, b * pages_per_block + u], axis=0)
            sums_ref[b] = total

    every_step = n_sum_steps == MOBA_HEADS * nb
    if every_step:
        reduce_pages()
    near_sel = jnp.logical_and(sel_ref[pl.ds(n_far, 1), :] > 0.5, i >= 1)
    m = absorb(m, n_far, snd_ref[block_rows(0), :], near_sel, 0.0)
    absorb(m, i, snd_ref[block_rows(1), :], True, 0.0)
    o_ref[...] = (acc_ref[:MOBA_DH, :] / acc_ref[MOBA_DH:MOBA_DH + 1, :]).T
    if not every_step:
        pl.when(step < n_sum_steps)(reduce_pages)


def _moba_prompt(q, k16, vt16, means, bnear_t, bdiag_t, cfar, cache_k, layer, page_table):
    t = q.shape[0]
    nb = t // MOBA_BLOCK
    n_seq, n_pages = page_table.shape
    n_sum_steps = n_seq * n_pages // PAGES_PER_STEP
    blocks_per_step = PAGES_PER_STEP // (MOBA_BLOCK // PAGE_SIZE)
    assert nb % FAR_BLOCKS == 0 and n_sum_steps <= MOBA_HEADS * nb
    bias_spec = pl.BlockSpec((None, MOBA_BLOCK, MOBA_BLOCK), lambda h, s, pt: (h, 0, 0))
    tile_spec = pl.BlockSpec((MOBA_BLOCK, MOBA_DH), lambda h, s, pt: (_query_block(s, nb), h))
    trip_logits = pltpu.VMEM((FAR_BLOCKS * MOBA_BLOCK, MOBA_BLOCK), F32)
    o_b, sums = pl.pallas_call(
        functools.partial(_moba_prompt_kernel, layer=layer, n_sum_steps=n_sum_steps),
        grid_spec=pltpu.PrefetchScalarGridSpec(
            num_scalar_prefetch=1,
            grid=(MOBA_HEADS, nb),
            in_specs=[pl.BlockSpec(memory_space=pltpu.SMEM),
                      tile_spec,
                      pl.BlockSpec((t, MOBA_DH), lambda h, s, pt: (0, h)),
                      pl.BlockSpec((None, VT_ROWS, t), lambda h, s, pt: (h, 0, 0)),
                      pl.BlockSpec((nb, MOBA_DH), lambda h, s, pt: (0, h)),
                      bias_spec, bias_spec,
                      pl.BlockSpec(memory_space=pl.ANY)],
            out_specs=[tile_spec,
                       pl.BlockSpec((None, blocks_per_step, MOBA_HEADS, MOBA_DH),
                                    lambda h, s, pt: (jnp.minimum(h * nb + s, n_sum_steps - 1), 0, 0, 0))],
            scratch_shapes=[pltpu.VMEM((nb, MOBA_BLOCK), F32), pltpu.VMEM((VT_ROWS, MOBA_BLOCK), F32),
                            trip_logits, trip_logits, pltpu.VMEM((2 * MOBA_BLOCK, MOBA_BLOCK), F32),
                            pltpu.VMEM((PAGE_SLOTS, PAGES_PER_STEP, PAGE_SIZE, MOBA_HEADS, MOBA_DH), F32),
                            pltpu.SemaphoreType.DMA((PAGE_SLOTS,))],
        ),
        out_shape=[jax.ShapeDtypeStruct((t, D_MODEL), F32),
                   jax.ShapeDtypeStruct((n_sum_steps, blocks_per_step, MOBA_HEADS, MOBA_DH), F32)],
        compiler_params=_cparams(2),
        name="moba_prompt",
    )(page_table.reshape(-1), cfar, q, k16, vt16, means, bnear_t, bdiag_t, cache_k)
    return o_b, sums.reshape(n_seq, n_pages * PAGE_SIZE // MOBA_BLOCK, MOBA_HEADS, MOBA_DH)


def _select_kernel(q_ref, sums_ref, out_ref):
    nbp = sums_ref.shape[1]
    ones = jnp.ones((MOBA_DH, LANE), BF16)
    for n in range(SELECT_SEQS):
        prod = (sums_ref[n] / MOBA_BLOCK) * q_ref[n]
        p_hi, p_mid, p_lo = _split3(prod.reshape(nbp * MOBA_HEADS, MOBA_DH))
        scores = _dot(p_hi, ones) + (_dot(p_mid, ones) + _dot(p_lo, ones))
        _, picks = _select_topk_rows(scores.reshape(nbp, MOBA_HEADS, LANE), True)
        for r, idx in enumerate(picks):
            out_ref[n, r] = idx[0]


def _select_blocks(q_heads, sums):
    n, nbp = sums.shape[:2]
    sb = SELECT_SEQS
    return pl.pallas_call(
        _select_kernel,
        grid=(n // sb,),
        in_specs=[pl.BlockSpec((sb, MOBA_HEADS, MOBA_DH), lambda s: (s, 0, 0)),
                  pl.BlockSpec((sb, nbp, MOBA_HEADS, MOBA_DH), lambda s: (s, 0, 0, 0))],
        out_specs=pl.BlockSpec((sb, MOBA_TOPK, MOBA_HEADS, LANE), lambda s: (s, 0, 0, 0)),
        out_shape=jax.ShapeDtypeStruct((n, MOBA_TOPK, MOBA_HEADS, LANE), jnp.int32),
        compiler_params=_cparams(1),
        name="select_blocks",
    )(q_heads, sums)


def _moba_decode_kernel(pages_ref, selblk_ref, bias_s_ref, q_ref, kn_ref, vn_ref, blast_ref,
                        ck_ref, cv_ref, o_ref, kbuf, vbuf, sems, *, layer, newest_block):
    pages_per_block = MOBA_BLOCK // PAGE_SIZE
    n_sel = MOBA_TOPK * pages_per_block
    seq = pl.program_id(0)
    slot = seq % 2

    def tile_copies(which_seq, which_slot):
        for h in range(MOBA_HEADS):
            for r in range(n_sel):
                page = pages_ref[(which_seq * MOBA_HEADS + h) * n_sel + r]
                dst_rows = pl.ds(r * PAGE_SIZE, PAGE_SIZE)
                yield pltpu.make_async_copy(ck_ref.at[layer, page, :, h, :], kbuf.at[which_slot, h, dst_rows],
                                            sems.at[which_slot])
                yield pltpu.make_async_copy(cv_ref.at[layer, page, :, h, :], vbuf.at[which_slot, h, dst_rows],
                                            sems.at[which_slot])

    @pl.when(seq == 0)
    def _():
        for c, cp in enumerate(tile_copies(seq, slot)):
            cp.start(priority=c % 2)

    @pl.when(seq + 1 < pl.num_programs(0))
    def _():
        for c, cp in enumerate(tile_copies(seq + 1, 1 - slot)):
            cp.start(priority=c % 2)

    for cp in tile_copies(seq, slot):
        cp.wait()

    logits, owns = [], []
    for h in range(MOBA_HEADS):
        q8 = jnp.broadcast_to(q_ref[h:h + 1, :] * (MOBA_DH ** -0.5), (SUBLANE, MOBA_DH))
        s = lax.dot_general(q8.astype(BF16), kbuf[slot, h].astype(BF16), (((1,), (1,)), ((), ())),
                            preferred_element_type=F32)
        bias = []
        for r in range(MOBA_TOPK):
            is_newest = selblk_ref[(seq * MOBA_HEADS + h) * MOBA_TOPK + r] == newest_block
            bias.append(jnp.where(is_newest, blast_ref[h], bias_s_ref[0, h]))
        logits.append(s + jnp.concatenate(bias, axis=-1))
        owns.append(jnp.sum(q8 * kn_ref[h:h + 1, :], axis=-1, keepdims=True) + bias_s_ref[1, h])
    probs = []
    for s, s_own in zip(logits, owns):
        m = jnp.maximum(s_own, jnp.max(s, axis=-1, keepdims=True))
        p = jnp.exp(s - m)
        p_own = jnp.exp(s_own - m)
        probs.append((p.astype(BF16), p_own, p_own + jnp.sum(p, axis=-1, keepdims=True)))
    for h, (p, p_own, l) in enumerate(probs):
        out = p_own * vn_ref[h:h + 1, :] + _dot(p, vbuf[slot, h].astype(BF16))
        o_ref[h:h + 1, :] = (out / l)[:1]


def _moba_decode(pages, selblk, bias_s, q_heads, kn_heads, vn_heads, blast, cache_k, cache_v, layer,
                 newest_block):
    n = q_heads.shape[0]
    n_sel = MOBA_TOPK * (MOBA_BLOCK // PAGE_SIZE)
    tok = pl.BlockSpec((None, MOBA_HEADS, MOBA_DH), lambda s, pg, sb: (s, 0, 0))
    tiles = pltpu.VMEM((2, MOBA_HEADS, n_sel * PAGE_SIZE, MOBA_DH), F32)
    return pl.pallas_call(
        functools.partial(_moba_decode_kernel, layer=layer, newest_block=newest_block),
        grid_spec=pltpu.PrefetchScalarGridSpec(
            num_scalar_prefetch=2,
            grid=(n,),
            in_specs=[pl.BlockSpec(memory_space=pltpu.SMEM), tok, tok, tok, _resident(blast.shape),
                      pl.BlockSpec(memory_space=pl.ANY), pl.BlockSpec(memory_space=pl.ANY)],
            out_specs=tok,
            scratch_shapes=[tiles, tiles, pltpu.SemaphoreType.DMA((2,))],
        ),
        out_shape=jax.ShapeDtypeStruct((n, MOBA_HEADS, MOBA_DH), F32),
        compiler_params=_cparams(1),
        name="moba_decode",
    )(pages, selblk, bias_s, q_heads, kn_heads, vn_heads, blast, cache_k, cache_v)


def _rel_bucket_table(n):
    d = np.arange(n)
    max_exact = REL_BUCKETS // 2
    big = max_exact + (np.log(np.maximum(d, max_exact).astype(np.float32) / max_exact)
                       / math.log(REL_MAX_DIST / max_exact) * (REL_BUCKETS - max_exact)).astype(np.int32)
    return np.where(d < max_exact, d, np.minimum(big, REL_BUCKETS - 1)).astype(np.int32)


def _bias_tables(rel_bias):
    bucket = _rel_bucket_table(2 * MOBA_BLOCK)
    assert bucket[MOBA_BLOCK + 1:].min() == REL_BUCKETS - 1
    one_hot = (bucket[:, None] == np.arange(REL_BUCKETS)[None, :]).astype(np.float32)
    by_dist = jnp.dot(one_hot, rel_bias, precision=lax.Precision.HIGHEST).T
    n = MOBA_BLOCK

    def toeplitz_t(g):
        g = jnp.pad(g, ((0, 0), (0, 1)))
        shifted = jnp.tile(g, (1, n))[:, :n * 2 * n].reshape(MOBA_HEADS, n, 2 * n)
        return shifted[:, :, n:]

    near_t = toeplitz_t(by_dist)
    diag_t = toeplitz_t(jnp.concatenate([jnp.full((MOBA_HEADS, n), NEG, F32), by_dist[:, :n]], axis=1))
    far = rel_bias[REL_BUCKETS - 1, :]
    last_blk = by_dist[:, n:0:-1]
    return near_t, diag_t, far, last_blk.reshape(MOBA_HEADS, 1, MOBA_BLOCK), rel_bias[0, :]


def kernel(x_prompt, x_sample, cache_k, cache_v, state_gla, page_table, w_in, w_alpha2, b_alpha,
           g_gla_norm, rel_bias, w_out, ln1_g, ln1_b, w_ffn_in, w_ffn_out, ln2_g, ln2_b):
    layer = 0
    n_batch, t, _ = x_prompt.shape
    n_seq, t_new, _ = x_sample.shape
    n_pages = page_table.shape[1]
    past = n_pages * PAGE_SIZE
    assert n_batch == 1 and t_new == 1 and t % MOBA_BLOCK == 0 and t % GLA_STEP == 0
    assert past % MOBA_BLOCK == 0 and past // MOBA_BLOCK >= MOBA_TOPK
    assert n_pages % PAGES_PER_STEP == 0 and n_seq % GLA_DECODE_SEQS == 0 and n_seq % SELECT_SEQS == 0
    nbp = past // MOBA_BLOCK

    hk = GLA_HEADS * GLA_DK
    lr0 = 2 * hk + 2 * D_MODEL
    w = w_in[layer]
    w_main = jnp.concatenate([w[:, :lr0], w[:, lr0 + GLA_LOWRANK:]], axis=1).astype(BF16)
    w_lr = jnp.pad(w[:, lr0:lr0 + GLA_LOWRANK], ((0, 0), (0, LANE - GLA_LOWRANK))).astype(BF16)
    wa2 = jnp.pad(w_alpha2[layer], ((0, LANE - GLA_LOWRANK), (0, 0))).astype(BF16)
    ba = b_alpha[layer].reshape(1, hk)
    g = g_gla_norm[layer].reshape(1, GLA_DV)
    wo = w_out[layer].astype(BF16)
    w1 = w_ffn_in[layer].astype(BF16)
    w2 = w_ffn_out[layer].astype(BF16)
    ln = [a[layer].reshape(1, D_MODEL) for a in (ln1_g, ln1_b, ln2_g, ln2_b)]
    near_t, diag_t, bias_far, bias_last, bias_own = _bias_tables(rel_bias)

    xp = x_prompt.reshape(t, D_MODEL)
    qk_a, v_a, r_a, q_b, k_b, v_b, g_a, g_b, lr = _in_proj(xp, w_main, w_lr)
    o_a, s_prompt = _gla_prompt(qk_a, v_a, r_a, lr, wa2, ba, g)
    k16, vt16, means = _moba_prep(k_b, v_b)
    o_b, sums = _moba_prompt(q_b, k16, vt16, means, near_t * LOG2E, diag_t * LOG2E, bias_far * LOG2E,
                             cache_k, layer, page_table)
    y_prompt = _out_ffn(xp, o_a, o_b, g_a, g_b, wo, w1, w2, *ln)

    xs = x_sample.reshape(n_seq, D_MODEL)
    qk_s, v_s, r_s, qb_s, kb_s, vb_s, ga_s, gb_s, lr_s = _in_proj(xs, w_main, w_lr)
    oa_s, s_sample = _gla_decode(qk_s, v_s, r_s, lr_s, wa2, ba, g, state_gla[layer])
    heads = lambda a: a.reshape(n_seq, MOBA_HEADS, MOBA_DH)
    picks = _select_blocks(heads(qb_s), sums)[..., 0]
    picks = picks.transpose(0, 2, 1)
    ppb = MOBA_BLOCK // PAGE_SIZE
    logical = (picks[..., None] * ppb + jnp.arange(ppb)).reshape(n_seq, -1)
    pages = jnp.take_along_axis(page_table, logical, axis=1).reshape(-1)
    bias_s = jnp.stack([bias_far, bias_own])
    ob_s = _moba_decode(pages, picks.reshape(-1), bias_s, heads(qb_s), heads(kb_s), heads(vb_s), bias_last,
                        cache_k, cache_v, layer, nbp - 1)
    y_sample = _out_ffn(xs, oa_s, ob_s.reshape(n_seq, D_MODEL), ga_s, gb_s, wo, w1, w2, *ln)

    kv_shape = lambda n, tt: (DEPTH, n, tt, MOBA_HEADS, MOBA_DH)
    return (y_prompt.reshape(1, t, D_MODEL), y_sample.reshape(n_seq, 1, D_MODEL),
            k_b.reshape(kv_shape(1, t)), v_b.reshape(kv_shape(1, t)),
            s_prompt.reshape(DEPTH, 1, GLA_HEADS, GLA_DK, GLA_DV),
            kb_s.reshape(kv_shape(n_seq, 1)), vb_s.reshape(kv_shape(n_seq, 1)),
            s_sample.reshape(DEPTH, n_seq, GLA_HEADS, GLA_DK, GLA_DV))
```
